```python
import math
import jax, jax.numpy as jnp
from jax import lax
import numpy as np

D_MODEL = 2048
BATCH = 2
SEQ = 16384
DEPTH = 2

D_ATTN = D_MODEL
D_SSM = D_MODEL
D_MIX = D_ATTN + D_SSM
HEAD_DIM = 128
N_HEADS_ATTN = D_ATTN // HEAD_DIM
N_KV_GROUPS = 2
HEADS_PER_GROUP = N_HEADS_ATTN // N_KV_GROUPS
KV_WIDTH = N_KV_GROUPS * HEAD_DIM
CMP_BLOCK = 32
CMP_STRIDE = 16
CMP_RATIO = CMP_BLOCK // CMP_STRIDE
CMP_HIDDEN = 256
SLC_BLOCK = 64
N_SELECT = 16
WINDOW = 512
Q_BLOCK = 128
ROPE_THETA = 10000.0
NEG_INF = -1e30
SEL_FORCED = 1e4
SSM_HEAD_DIM = 64
N_HEADS_SSM = D_SSM // SSM_HEAD_DIM
N_SSM_GROUPS = 8
SSM_HEADS_PER_GROUP = N_HEADS_SSM // N_SSM_GROUPS
D_STATE = 128
CONV_WIDTH = 4
CONV_DIM = D_SSM + 2 * N_SSM_GROUPS * D_STATE
SSD_CHUNK = 256
PROJ_WIDTHS = (D_ATTN, KV_WIDTH, KV_WIDTH, KV_WIDTH, KV_WIDTH, KV_WIDTH, KV_WIDTH,
               3 * N_HEADS_ATTN, D_ATTN, CONV_DIM, N_HEADS_SSM, D_SSM)
D_PROJ = sum(PROJ_WIDTHS)
POS_OFFSET_MAX = 4096

kernel_name = "hymba_nsa_mamba2_hybrid"


def rmsnorm(x, w, eps=1e-6):
    xf = x.astype(jnp.float32)
    y = xf * lax.rsqrt(jnp.mean(xf * xf, axis=-1, keepdims=True) + eps)
    return y.astype(x.dtype) * w


def grouped_rmsnorm(x, w, n_groups, eps=1e-6):
    shp = x.shape
    xg = x.reshape(shp[:-1] + (n_groups, shp[-1] // n_groups)).astype(jnp.float32)
    y = xg * lax.rsqrt(jnp.mean(xg * xg, axis=-1, keepdims=True) + eps)
    return y.reshape(shp).astype(x.dtype) * w


def rope(u, pos):
    half = u.shape[-1] // 2
    inv_freq = ROPE_THETA ** (-jnp.arange(half, dtype=jnp.float32) / half)
    ang = pos.astype(jnp.float32)[..., None] * inv_freq
    cos = jnp.cos(ang)[:, :, None, :]
    sin = jnp.sin(ang)[:, :, None, :]
    u1 = u[..., :half].astype(jnp.float32)
    u2 = u[..., half:].astype(jnp.float32)
    return jnp.concatenate([u1 * cos - u2 * sin, u2 * cos + u1 * sin], axis=-1).astype(u.dtype)


def split_columns(proj):
    offsets = np.cumsum(PROJ_WIDTHS)[:-1].tolist()
    return jnp.split(proj, offsets, axis=-1)


def compress_blocks(u, pe, w1, w2):
    b, s, g, d = u.shape
    n_chunks = s // CMP_STRIDE
    n_cmp = n_chunks - CMP_RATIO + 1
    chunks = u.reshape(b, n_chunks, CMP_STRIDE, g, d)
    pe = pe.reshape(CMP_RATIO, CMP_STRIDE, 1, d)
    w1 = w1.reshape(CMP_RATIO, CMP_STRIDE, d, CMP_HIDDEN)
    hid = sum(jnp.einsum("bcigd,idh->bcgh", chunks[:, r:r + n_cmp] + pe[r], w1[r])
              for r in range(CMP_RATIO))
    return jnp.einsum("bcgh,hd->bcgd", jax.nn.silu(hid), w2)


def nsa_attention(q, k_cmp, v_cmp, k_slc, v_slc, k_win, v_win, gates):
    b, s = q.shape[:2]
    g, r, dk = N_KV_GROUPS, HEADS_PER_GROUP, HEAD_DIM
    n_cmp = k_cmp.shape[1]
    n_slc = s // SLC_BLOCK
    n_sel = min(N_SELECT, n_slc)
    slc_ratio = SLC_BLOCK // CMP_STRIDE
    n_overlap = slc_ratio + CMP_RATIO - 1
    pad_left = CMP_RATIO - 1
    pad_right = slc_ratio * (n_slc - 1) + n_overlap - (n_cmp + pad_left)
    scale = dk ** -0.5
    qg = q.reshape(b, s, g, r, dk)
    gates = gates.reshape(b, s, g, r, 3)
    kb = jnp.moveaxis(k_slc.reshape(b, n_slc, SLC_BLOCK, g, dk), 3, 1)
    vb = jnp.moveaxis(v_slc.reshape(b, n_slc, SLC_BLOCK, g, dk), 3, 1)
    k_wp = jnp.pad(k_win, ((0, 0), (WINDOW, 0), (0, 0), (0, 0)))
    v_wp = jnp.pad(v_win, ((0, 0), (WINDOW, 0), (0, 0), (0, 0)))
    cmp_end = jnp.arange(n_cmp) * CMP_STRIDE + CMP_BLOCK - 1
    blk_ids = jnp.arange(n_slc)
    b_ix = jnp.arange(b)[:, None, None, None]
    g_ix = jnp.arange(g)[None, :, None, None]

    def query_block(qi):
        t0 = qi * Q_BLOCK
        qb = lax.dynamic_slice_in_dim(qg, t0, Q_BLOCK, axis=1)
        t = t0 + jnp.arange(Q_BLOCK)
        s_c = jnp.einsum("bqgrd,bngd->bgrqn", qb, k_cmp).astype(jnp.float32) * scale
        m_c = cmp_end[None, :] <= t[:, None]
        p_c = jax.nn.softmax(jnp.where(m_c, s_c, NEG_INF), axis=-1) * m_c
        o_c = jnp.einsum("bgrqn,bngd->bqgrd", p_c.astype(v_cmp.dtype), v_cmp)
        imp = jnp.pad(p_c.sum(axis=2), ((0, 0), (0, 0), (0, 0), (pad_left, pad_right)))
        imp_s = sum(imp[..., m:m + slc_ratio * (n_slc - 1) + 1:slc_ratio] for m in range(n_overlap))
        cur = t // SLC_BLOCK
        forced = (blk_ids[None] == 0) | (blk_ids[None] == cur[:, None]) | (blk_ids[None] == cur[:, None] - 1)
        visible = blk_ids[None] <= cur[:, None]
        score = jnp.where(forced, SEL_FORCED, jnp.where(visible, imp_s, -SEL_FORCED))
        _, idx = lax.top_k(score, n_sel)
        ks = kb[b_ix, g_ix, idx]
        vs = vb[b_ix, g_ix, idx]
        s_s = jnp.einsum("bqgrd,bgqnld->bgrqnl", qb, ks).astype(jnp.float32) * scale
        tok = idx[..., None] * SLC_BLOCK + jnp.arange(SLC_BLOCK)
        m_s = (tok <= t[None, None, :, None, None])[:, :, None]
        s_s = jnp.where(m_s, s_s, NEG_INF).reshape(b, g, r, Q_BLOCK, n_sel * SLC_BLOCK)
        p_s = jax.nn.softmax(s_s, axis=-1)
        o_s = jnp.einsum("bgrqm,bgqmd->bqgrd", p_s.astype(vs.dtype),
                         vs.reshape(b, g, Q_BLOCK, n_sel * SLC_BLOCK, dk))
        kw = lax.dynamic_slice_in_dim(k_wp, t0, WINDOW + Q_BLOCK, axis=1)
        vw = lax.dynamic_slice_in_dim(v_wp, t0, WINDOW + Q_BLOCK, axis=1)
        pos_k = t0 - WINDOW + jnp.arange(WINDOW + Q_BLOCK)
        m_w = ((pos_k[None] <= t[:, None]) & (pos_k[None] > t[:, None] - WINDOW) & (pos_k[None] >= 0))
        s_w = jnp.einsum("bqgrd,bkgd->bgrqk", qb, kw).astype(jnp.float32) * scale
        p_w = jax.nn.softmax(jnp.where(m_w, s_w, NEG_INF), axis=-1)
        o_w = jnp.einsum("bgrqk,bkgd->bqgrd", p_w.astype(vw.dtype), vw)
        gb = lax.dynamic_slice_in_dim(gates, t0, Q_BLOCK, axis=1)
        o = gb[..., 0:1] * o_c + gb[..., 1:2] * o_s + gb[..., 2:3] * o_w
        return o.astype(q.dtype)

    out = lax.map(query_block, jnp.arange(s // Q_BLOCK))
    return jnp.moveaxis(out, 0, 1).reshape(b, s, N_HEADS_ATTN * dk)


def causal_depthwise_conv(u, w, bias):
    y = lax.conv_general_dilated(u, w[:, None, :], window_strides=(1,), padding=[(CONV_WIDTH - 1, 0)],
                                 dimension_numbers=("NWC", "WIO", "NWC"), feature_group_count=u.shape[-1])
    return y + bias


def ssd_chunked(x, dt, a, bmat, cmat):
    b, s = x.shape[:2]
    chunk = math.gcd(s, SSD_CHUNK)
    nc = s // chunk
    gs, r = N_SSM_GROUPS, SSM_HEADS_PER_GROUP
    xdt = (x * dt[..., None]).reshape(b, nc, chunk, gs, r, SSM_HEAD_DIM)
    da = (dt * a).reshape(b, nc, chunk, gs, r)
    bc = bmat.reshape(b, nc, chunk, gs, D_STATE)
    cc = cmat.reshape(b, nc, chunk, gs, D_STATE)
    causal = jnp.tril(jnp.ones((chunk, chunk), bool))[None, :, :, None, None]

    def step(state, inp):
        xk, dak, bk, ck = inp
        cum = jnp.cumsum(dak, axis=1)
        decay = jnp.exp(jnp.where(causal, cum[:, :, None] - cum[:, None], -jnp.inf))
        w_diag = jnp.einsum("blgn,bsgn->blsg", ck, bk)[..., None] * decay
        y = jnp.einsum("blsgr,bsgrp->blgrp", w_diag, xk)
        y = y + jnp.einsum("blgn,bgrpn->blgrp", ck, state) * jnp.exp(cum)[..., None]
        last = cum[:, -1]
        x_end = jnp.exp(last[:, None] - cum)[..., None] * xk
        state = state * jnp.exp(last)[..., None, None] + jnp.einsum("blgn,blgrp->bgrpn", bk, x_end)
        return state, y

    state0 = jnp.zeros((b, gs, r, SSM_HEAD_DIM, D_STATE), jnp.float32)
    xs = (jnp.moveaxis(xdt, 1, 0), jnp.moveaxis(da, 1, 0), jnp.moveaxis(bc, 1, 0), jnp.moveaxis(cc, 1, 0))
    _, y = lax.scan(step, state0, xs)
    return jnp.moveaxis(y, 0, 1).reshape(b, s, N_HEADS_SSM, SSM_HEAD_DIM)


def hybrid_layer(x, positions, pos_cmp, norm_w, w_in, pe_k, w1_k, w2_k, pe_v, w1_v, w2_v,
                 conv_w, conv_b, dt_bias, a_log, d_skip, norm_attn_w, norm_ssm_w, w_out):
    b, s, _ = x.shape
    h = rmsnorm(x, norm_w)
    proj = jnp.einsum("bsd,dp->bsp", h, w_in)
    (q, kc, vc, ks, vs, kw, vw, gate, z_attn, xbc, dt, z_ssm) = split_columns(proj)
    kv_shape = (b, s, N_KV_GROUPS, HEAD_DIM)
    q = rope(q.reshape(b, s, N_HEADS_ATTN, HEAD_DIM), positions)
    k_slc = rope(ks.reshape(kv_shape), positions)
    k_win = rope(kw.reshape(kv_shape), positions)
    k_cmp = rope(compress_blocks(kc.reshape(kv_shape), pe_k, w1_k, w2_k), pos_cmp)
    v_cmp = compress_blocks(vc.reshape(kv_shape), pe_v, w1_v, w2_v)
    gates = jax.nn.sigmoid(gate.astype(jnp.float32)).reshape(b, s, N_HEADS_ATTN, 3)
    o_attn = nsa_attention(q, k_cmp, v_cmp, k_slc, vs.reshape(kv_shape), k_win, vw.reshape(kv_shape), gates)
    y_attn = rmsnorm(o_attn * jax.nn.silu(z_attn), norm_attn_w)
    xbc = jax.nn.silu(causal_depthwise_conv(xbc, conv_w, conv_b))
    xs_, bm, cm = jnp.split(xbc, [D_SSM, D_SSM + N_SSM_GROUPS * D_STATE], axis=-1)
    xs_ = xs_.reshape(b, s, N_HEADS_SSM, SSM_HEAD_DIM).astype(jnp.float32)
    dt = jax.nn.softplus(dt.astype(jnp.float32) + dt_bias.astype(jnp.float32))
    a = -jnp.exp(a_log.astype(jnp.float32))
    y = ssd_chunked(xs_, dt, a,
                    bm.reshape(b, s, N_SSM_GROUPS, D_STATE).astype(jnp.float32),
                    cm.reshape(b, s, N_SSM_GROUPS, D_STATE).astype(jnp.float32))
    y = (y + d_skip.astype(jnp.float32)[:, None] * xs_).reshape(b, s, D_SSM).astype(x.dtype)
    y_ssm = grouped_rmsnorm(y * jax.nn.silu(z_ssm), norm_ssm_w, N_SSM_GROUPS)
    mixed = jnp.concatenate([y_attn, y_ssm], axis=-1)
    return x + jnp.einsum("bsm,md->bsd", mixed, w_out)


def setup_inputs(seed: int = 0) -> dict:
    key = jax.random.key(seed)
    ks = jax.random.split(key, 24)
    f32 = jnp.float32

    def nrm(k, shape, scale):
        return jax.random.normal(k, shape, f32) * scale

    x = nrm(ks[0], (BATCH, SEQ, D_MODEL), 1.0)
    positions = (jax.random.randint(ks[1], (BATCH, 1), 0, POS_OFFSET_MAX, dtype=jnp.int32)
                 + jnp.arange(SEQ, dtype=jnp.int32)[None])
    norm_w = 1.0 + nrm(ks[2], (DEPTH, D_MODEL), 0.02)
    w_in = nrm(ks[3], (DEPTH, D_MODEL, D_PROJ), D_MODEL ** -0.5)
    cmp_pe_k = nrm(ks[4], (DEPTH, CMP_BLOCK, HEAD_DIM), 0.1)
    cmp_w1_k = nrm(ks[5], (DEPTH, CMP_BLOCK, HEAD_DIM, CMP_HIDDEN), (CMP_BLOCK * HEAD_DIM) ** -0.5)
    cmp_w2_k = nrm(ks[6], (DEPTH, CMP_HIDDEN, HEAD_DIM), CMP_HIDDEN ** -0.5)
    cmp_pe_v = nrm(ks[7], (DEPTH, CMP_BLOCK, HEAD_DIM), 0.1)
    cmp_w1_v = nrm(ks[8], (DEPTH, CMP_BLOCK, HEAD_DIM, CMP_HIDDEN), (CMP_BLOCK * HEAD_DIM) ** -0.5)
    cmp_w2_v = nrm(ks[9], (DEPTH, CMP_HIDDEN, HEAD_DIM), CMP_HIDDEN ** -0.5)
    conv_w = nrm(ks[10], (DEPTH, CONV_WIDTH, CONV_DIM), CONV_WIDTH ** -0.5)
    conv_b = nrm(ks[11], (DEPTH, CONV_DIM), 0.02)
    dt0 = jnp.exp(jax.random.uniform(ks[12], (DEPTH, N_HEADS_SSM), f32, math.log(1e-3), math.log(1e-1)))
    dt_bias = dt0 + jnp.log(-jnp.expm1(-dt0))
    a_log = jnp.log(jax.random.uniform(ks[13], (DEPTH, N_HEADS_SSM), f32, 1.0, 16.0))
    d_skip = 1.0 + nrm(ks[14], (DEPTH, N_HEADS_SSM), 0.02)
    norm_attn_w = 1.0 + nrm(ks[15], (DEPTH, D_ATTN), 0.02)
    norm_ssm_w = 1.0 + nrm(ks[16], (DEPTH, D_SSM), 0.02)
    w_out = nrm(ks[17], (DEPTH, D_MIX, D_MODEL), D_MIX ** -0.5)
    final_norm_w = 1.0 + nrm(ks[18], (D_MODEL,), 0.02)
    return {"x": x, "positions": positions, "norm_w": norm_w, "w_in": w_in,
            "cmp_pe_k": cmp_pe_k, "cmp_w1_k": cmp_w1_k, "cmp_w2_k": cmp_w2_k,
            "cmp_pe_v": cmp_pe_v, "cmp_w1_v": cmp_w1_v, "cmp_w2_v": cmp_w2_v,
            "conv_w": conv_w, "conv_b": conv_b, "dt_bias": dt_bias, "a_log": a_log,
            "d_skip": d_skip, "norm_attn_w": norm_attn_w, "norm_ssm_w": norm_ssm_w,
            "w_out": w_out, "final_norm_w": final_norm_w}


def reference(x, positions, norm_w, w_in, cmp_pe_k, cmp_w1_k, cmp_w2_k, cmp_pe_v, cmp_w1_v, cmp_w2_v,
              conv_w, conv_b, dt_bias, a_log, d_skip, norm_attn_w, norm_ssm_w, w_out, final_norm_w):
    pos_cmp = positions[:, CMP_BLOCK - 1::CMP_STRIDE]
    for l in range(DEPTH):
        x = hybrid_layer(x, positions, pos_cmp, norm_w[l], w_in[l],
                         cmp_pe_k[l], cmp_w1_k[l], cmp_w2_k[l], cmp_pe_v[l], cmp_w1_v[l], cmp_w2_v[l],
                         conv_w[l], conv_b[l], dt_bias[l], a_log[l], d_skip[l],
                         norm_attn_w[l], norm_ssm_w[l], w_out[l])
    return rmsnorm(x, final_norm_w)
```

```python
import functools

import jax
import jax.numpy as jnp
from jax import lax
from jax.experimental import pallas as pl
from jax.experimental.pallas import tpu as pltpu

F32 = jnp.float32
BF16 = jnp.bfloat16

D_MODEL = 2048
HEAD_DIM = 128
N_HEADS_ATTN = 16
N_KV_GROUPS = 2
HEADS_PER_GROUP = 8
GROUP_WIDTH = HEADS_PER_GROUP * HEAD_DIM
CMP_STRIDE = 16
CMP_BLOCK = 32
CMP_HIDDEN = 256
SLC_BLOCK = 64
N_SELECT = 16
WINDOW = 512
Q_BLOCK = 128
ROPE_THETA = 10000.0
NEG_INF = -1e30
SEL_FORCED = 1e4
N_HEADS_SSM = 32
SSM_HEAD_DIM = 64
N_SSM_GROUPS = 8
SSM_HEADS_PER_GROUP = 4
D_STATE = 128
D_SSM = 2048
CONV_WIDTH = 4
CONV_DIM = 4096
SSD_CHUNK = 256
EPS = 1e-6
SCALE = HEAD_DIM ** -0.5

KV_COLS = 6 * N_KV_GROUPS * HEAD_DIM
GD_BLOCK = KV_COLS // 128
GATE_LANES = 3 * N_HEADS_ATTN
DT_LANE0 = GATE_LANES
D_PACK = 12288
PROJ_TM = 512
PROJ_TN = 2048
PREP_TQ = 256
OUT_TM = 256
CMP_CHUNK = 128
SEL_CHUNK = 256
WIN_SPAN = WINDOW + Q_BLOCK
VMEM_LIMIT = 56 * 1024 * 1024


def _dot(a, b):
    return jnp.dot(a, b, preferred_element_type=F32)


def _dot_nt(a, b):
    return lax.dot_general(a, b, (((1,), (1,)), ((), ())), preferred_element_type=F32)


def _split3(x):
    hi = x.astype(BF16)
    r1 = x - hi.astype(F32)
    mid = r1.astype(BF16)
    lo = (r1 - mid.astype(F32)).astype(BF16)
    return hi, mid, lo


def _sigmoid(x):
    return 1.0 / (1.0 + jnp.exp(-x))


def _params(sem):
    return pltpu.CompilerParams(dimension_semantics=sem, vmem_limit_bytes=VMEM_LIMIT)


def _rope_table_kernel(pos_ref, inv_ref, cos_ref, sin_ref):
    ang = pos_ref[0] * inv_ref[...]
    lane = lax.broadcasted_iota(jnp.int32, ang.shape, 1)
    cos_ref[0] = jnp.cos(ang)
    s = jnp.sin(ang)
    sin_ref[0] = jnp.where(lane < HEAD_DIM // 2, -s, s)


def _rope_tables(pos_f, inv_full):
    b, n, _ = pos_f.shape
    rows = n if n <= 2048 else 2048
    out = jax.ShapeDtypeStruct((b, n, HEAD_DIM), F32)
    return pl.pallas_call(
        _rope_table_kernel,
        grid=(b, n // rows),
        in_specs=[pl.BlockSpec((1, rows, 1), lambda i, j: (i, j, 0)),
                  pl.BlockSpec((1, HEAD_DIM), lambda i, j: (0, 0))],
        out_specs=[pl.BlockSpec((1, rows, HEAD_DIM), lambda i, j: (i, j, 0))] * 2,
        out_shape=[out, out],
        compiler_params=_params(("parallel", "parallel")),
        name="rope_tables",
    )(pos_f, inv_full)


def _proj_kernel(x_ref, nw_ref, w_ref, o_ref, h_ref):
    @pl.when(pl.program_id(1) == 0)
    def _():
        x = x_ref[...]
        ms = jnp.mean(x * x, axis=-1, keepdims=True)
        h_ref[...] = (x * lax.rsqrt(ms + EPS) * nw_ref[...]).astype(BF16)

    o_ref[...] = _dot(h_ref[...], w_ref[...])


def _in_proj(x2d, norm_w, w_pack):
    t = x2d.shape[0]
    return pl.pallas_call(
        _proj_kernel,
        grid=(t // PROJ_TM, D_PACK // PROJ_TN),
        in_specs=[pl.BlockSpec((PROJ_TM, D_MODEL), lambda i, j: (i, 0)),
                  pl.BlockSpec((1, D_MODEL), lambda i, j: (0, 0)),
                  pl.BlockSpec((D_MODEL, PROJ_TN), lambda i, j: (0, j))],
        out_specs=pl.BlockSpec((PROJ_TM, PROJ_TN), lambda i, j: (i, j)),
        out_shape=jax.ShapeDtypeStruct((t, D_PACK), F32),
        scratch_shapes=[pltpu.VMEM((PROJ_TM, D_MODEL), BF16)],
        compiler_params=_params(("parallel", "arbitrary")),
        name="in_proj",
    )(x2d, norm_w, w_pack)


def _prep_kernel(kv_ref, q_ref, cos_ref, sin_ref,
                 qo_ref, kcvc_ref, ks_ref, kw_ref, vst_ref, vwt_ref):
    c = cos_ref[0]
    s = sin_ref[0]

    def rope(u):
        return u * c + pltpu.roll(u, HEAD_DIM // 2, 1) * s

    for h in range(N_HEADS_ATTN):
        sl = slice(HEAD_DIM * h, HEAD_DIM * (h + 1))
        qo_ref[0, :, sl] = (rope(q_ref[:, sl]) * SCALE).astype(BF16)

    def seg(i, g):
        lo = (i * N_KV_GROUPS + g) * HEAD_DIM
        return kv_ref[:, lo:lo + HEAD_DIM]

    for g in range(N_KV_GROUPS):
        kcvc_ref[0, 0, g] = seg(0, g).astype(BF16)
        kcvc_ref[0, 1, g] = seg(1, g).astype(BF16)
        ks_ref[0, g] = rope(seg(2, g)).astype(BF16)
        vst_ref[0, g] = seg(3, g).T.astype(BF16)
        kw_ref[0, g] = rope(seg(4, g)).astype(BF16)
        vwt_ref[0, g] = seg(5, g).T.astype(BF16)


def _prep(proj, cos_t, sin_t, b, s):
    nq = s // PREP_TQ
    g = N_KV_GROUPS
    row = lambda i, j: (i * nq + j, 0)
    kshape = jax.ShapeDtypeStruct((b, g, s, HEAD_DIM), BF16)
    vshape = jax.ShapeDtypeStruct((b, g, HEAD_DIM, s), BF16)
    kspec = pl.BlockSpec((1, g, PREP_TQ, HEAD_DIM), lambda i, j: (i, 0, j, 0))
    vspec = pl.BlockSpec((1, g, HEAD_DIM, PREP_TQ), lambda i, j: (i, 0, 0, j))
    return pl.pallas_call(
        _prep_kernel,
        grid=(b, nq),
        in_specs=[pl.BlockSpec((PREP_TQ, KV_COLS), row),
                  pl.BlockSpec((PREP_TQ, D_MODEL), lambda i, j: (i * nq + j, 1)),
                  pl.BlockSpec((1, PREP_TQ, HEAD_DIM), lambda i, j: (i, j, 0)),
                  pl.BlockSpec((1, PREP_TQ, HEAD_DIM), lambda i, j: (i, j, 0))],
        out_specs=[pl.BlockSpec((1, PREP_TQ, D_MODEL), lambda i, j: (i, j, 0)),
                   pl.BlockSpec((1, 2, g, PREP_TQ, HEAD_DIM), lambda i, j: (i, 0, 0, j, 0)),
                   kspec, kspec, vspec, vspec],
        out_shape=[jax.ShapeDtypeStruct((b, s, D_MODEL), BF16),
                   jax.ShapeDtypeStruct((b, 2, g, s, HEAD_DIM), BF16),
                   kshape, kshape, vshape, vshape],
        compiler_params=_params(("parallel", "parallel")),
        name="rope_relayout",
    )(proj, proj, cos_t, sin_t)


def _cmp_kernel(x_ref, w1_ref, w2_ref, pe_ref, cos_ref, sin_ref, kc_ref, vct_ref):
    n = x_ref.shape[3]
    for kv in range(2):
        x = x_ref[0, kv, 0]
        a0 = _dot(x, w1_ref[kv, 0])
        a1 = _dot(x, w1_ref[kv, 1])
        bias = jnp.zeros((8, CMP_HIDDEN), F32)
        for r in range(2):
            pe = jnp.broadcast_to(pe_ref[kv, r], (8, CMP_STRIDE * HEAD_DIM))
            for part in _split3(pe):
                bias = bias + _dot(part, w1_ref[kv, r])
        hid = a0 + pltpu.roll(a1, n - 1, 0) + bias[0:1]
        act = hid * _sigmoid(hid)
        out = _dot(act.astype(BF16), w2_ref[kv])
        if kv == 0:
            roped = out * cos_ref[0] + pltpu.roll(out, HEAD_DIM // 2, 1) * sin_ref[0]
            kc_ref[0, 0] = roped.astype(BF16)
        else:
            vct_ref[0, 0] = out.T.astype(BF16)


def _compress(kcvc, w1, w2, pe, cos_c, sin_c):
    b, _, g, n, width = kcvc.shape
    return pl.pallas_call(
        _cmp_kernel,
        grid=(b, g),
        in_specs=[pl.BlockSpec((1, 2, 1, n, width), lambda i, j: (i, 0, j, 0, 0)),
                  pl.BlockSpec(w1.shape, lambda i, j: (0, 0, 0, 0)),
                  pl.BlockSpec(w2.shape, lambda i, j: (0, 0, 0)),
                  pl.BlockSpec(pe.shape, lambda i, j: (0, 0, 0, 0)),
                  pl.BlockSpec((1, n, HEAD_DIM), lambda i, j: (i, 0, 0)),
                  pl.BlockSpec((1, n, HEAD_DIM), lambda i, j: (i, 0, 0))],
        out_specs=[pl.BlockSpec((1, 1, n, HEAD_DIM), lambda i, j: (i, j, 0, 0)),
                   pl.BlockSpec((1, 1, HEAD_DIM, n), lambda i, j: (i, j, 0, 0))],
        out_shape=[jax.ShapeDtypeStruct((b, g, n, HEAD_DIM), BF16),
                   jax.ShapeDtypeStruct((b, g, HEAD_DIM, n), BF16)],
        compiler_params=_params(("parallel", "parallel")),
        name="compress",
    )(kcvc, w1, w2, pe, cos_c, sin_c)


def _attn_kernel(q_ref, kc_ref, vct_ref, ks_ref, vst_ref, kw_ref, vwt_ref, gd_ref, o_ref,
                 sc_ref, imp_ref, sel_ref, acc_ref, gt_ref):
    n_cmp = kc_ref.shape[2]
    n_blk = sel_ref.shape[0]
    g = pl.program_id(1)
    qi = pl.program_id(2)
    t0 = qi * Q_BLOCK
    lanes_all = GROUP_WIDTH

    qs = jnp.concatenate(
        [q_ref[0, :, HEAD_DIM * r:HEAD_DIM * (r + 1)] for r in range(HEADS_PER_GROUP)], axis=0)
    t_lane = t0 + lax.broadcasted_iota(jnp.int32, (1, Q_BLOCK), 1)

    def tile_heads(x, n):
        return jnp.concatenate([x] * n, axis=1)

    n_c = (8 * qi + 6) // CMP_CHUNK + 1
    imp_ref[...] = jnp.zeros_like(imp_ref)
    acc_ref[...] = jnp.zeros_like(acc_ref)

    def cmp_scores(c, m):
        off = pl.multiple_of(c * CMP_CHUNK, CMP_CHUNK)
        s = _dot_nt(kc_ref[0, 0, pl.ds(off, CMP_CHUNK), :], qs)
        ci = off + lax.broadcasted_iota(jnp.int32, (CMP_CHUNK, Q_BLOCK), 0)
        vis = ci * CMP_STRIDE + (CMP_BLOCK - 1) <= t_lane
        s = s + tile_heads(jnp.where(vis, 0.0, NEG_INF), HEADS_PER_GROUP)
        sc_ref[pl.ds(off, CMP_CHUNK), :] = s
        return jnp.maximum(m, jnp.max(s, axis=0, keepdims=True))

    m_c = lax.fori_loop(0, n_c, cmp_scores, jnp.full((1, lanes_all), NEG_INF, F32))

    def cmp_exp(c, l):
        off = pl.multiple_of(c * CMP_CHUNK, CMP_CHUNK)
        s = sc_ref[pl.ds(off, CMP_CHUNK), :]
        e = jnp.where(s > 0.5 * NEG_INF, jnp.exp(s - m_c), 0.0)
        sc_ref[pl.ds(off, CMP_CHUNK), :] = e
        return l + jnp.sum(e, axis=0, keepdims=True)

    l_c = lax.fori_loop(0, n_c, cmp_exp, jnp.zeros((1, lanes_all), F32))
    inv_c = 1.0 / jnp.where(l_c > 0.0, l_c, 1.0)

    def cmp_out(c, carry):
        off = pl.multiple_of(c * CMP_CHUNK, CMP_CHUNK)
        pn = sc_ref[pl.ds(off, CMP_CHUNK), :] * inv_c
        imp = pn[:, 0:Q_BLOCK]
        for r in range(1, HEADS_PER_GROUP):
            imp = imp + pn[:, Q_BLOCK * r:Q_BLOCK * (r + 1)]
        imp_ref[pl.ds(off, CMP_CHUNK), :] = imp
        acc_ref[0] += _dot(vct_ref[0, 0, :, pl.ds(off, CMP_CHUNK)], pn.astype(BF16))
        return carry

    lax.fori_loop(0, n_c, cmp_out, 0)

    ratio = SLC_BLOCK // CMP_STRIDE
    score_in = pltpu.roll(imp_ref[pl.ds(ratio - 1, n_blk, stride=ratio), :], 1, 0)
    for m in range(ratio):
        score_in = score_in + imp_ref[pl.ds(m, n_blk, stride=ratio), :]
    blk = lax.broadcasted_iota(jnp.int32, (n_blk, Q_BLOCK), 0)
    cur = t_lane // SLC_BLOCK
    forced = (blk == 0) | (blk == cur) | (blk == cur - 1)
    visible = blk <= cur
    score0 = jnp.where(forced, SEL_FORCED, jnp.where(visible, score_in, -SEL_FORCED))
    blk_f = blk.astype(F32)

    def pick(_, carry):
        score, sel = carry
        mx = jnp.max(score, axis=0, keepdims=True)
        first = jnp.min(jnp.where(score == mx, blk_f, float(n_blk)), axis=0, keepdims=True)
        hit = blk_f == first
        return jnp.where(hit, -jnp.inf, score), jnp.where(hit, 1.0, sel)

    _, sel = lax.fori_loop(0, N_SELECT, pick, (score0, jnp.zeros((n_blk, Q_BLOCK), F32)))
    sel_ref[...] = jnp.where(visible, sel, 0.0)

    n_k = (t0 + Q_BLOCK - 1) // SEL_CHUNK + 1
    blocks_per_chunk = SEL_CHUNK // SLC_BLOCK
    pair = 2 * Q_BLOCK

    def sel_step(c, carry):
        m, l = carry
        off = pl.multiple_of(c * SEL_CHUNK, SEL_CHUNK)
        k = ks_ref[0, 0, pl.ds(off, SEL_CHUNK), :]
        vt = vst_ref[0, 0, :, pl.ds(off, SEL_CHUNK)]
        rows = [jnp.broadcast_to(sel_ref[pl.ds(c * blocks_per_chunk + j, 1), :], (SLC_BLOCK, Q_BLOCK))
                for j in range(blocks_per_chunk)]
        picked = jnp.concatenate(rows, axis=0)
        kpos = off + lax.broadcasted_iota(jnp.int32, (SEL_CHUNK, Q_BLOCK), 0)
        bias = tile_heads(jnp.where((picked > 0.5) & (kpos <= t_lane), 0.0, NEG_INF), 2)
        ms, ls = [], []
        for hp in range(HEADS_PER_GROUP // 2):
            sl = slice(pair * hp, pair * (hp + 1))
            s = _dot_nt(k, qs[sl]) + bias
            m_old = m[:, sl]
            m_new = jnp.maximum(m_old, jnp.max(s, axis=0, keepdims=True))
            alpha = jnp.exp(m_old - m_new)
            p = jnp.exp(s - m_new)
            ls.append(alpha * l[:, sl] + jnp.sum(p, axis=0, keepdims=True))
            ms.append(m_new)
            acc_ref[1, :, sl] = alpha * acc_ref[1, :, sl] + _dot(vt, p.astype(BF16))
        return jnp.concatenate(ms, axis=1), jnp.concatenate(ls, axis=1)

    _, l_s = lax.fori_loop(0, n_k, sel_step,
                           (jnp.full((1, lanes_all), NEG_INF, F32), jnp.zeros((1, lanes_all), F32)))
    inv_s = 1.0 / l_s

    ws = pl.multiple_of(jnp.maximum(t0 - WINDOW, 0), Q_BLOCK)
    kw = kw_ref[0, 0, pl.ds(ws, WIN_SPAN), :]
    vwt = vwt_ref[0, 0, :, pl.ds(ws, WIN_SPAN)]
    kpos = ws + lax.broadcasted_iota(jnp.int32, (WIN_SPAN, Q_BLOCK), 0)
    in_win = (kpos <= t_lane) & (kpos > t_lane - WINDOW)
    wbias = tile_heads(jnp.where(in_win, 0.0, NEG_INF), 2)
    for hp in range(HEADS_PER_GROUP // 2):
        sl = slice(pair * hp, pair * (hp + 1))
        s = _dot_nt(kw, qs[sl]) + wbias
        p = jnp.exp(s - jnp.max(s, axis=0, keepdims=True))
        inv_w = 1.0 / jnp.sum(p, axis=0, keepdims=True)
        acc_ref[2, :, sl] = _dot(vwt, p.astype(BF16)) * inv_w

    gt_ref[...] = _sigmoid(gd_ref[...]).T
    for r in range(HEADS_PER_GROUP):
        sl = slice(Q_BLOCK * r, Q_BLOCK * (r + 1))
        base = 3 * (HEADS_PER_GROUP * g + r)
        o_t = (gt_ref[pl.ds(base, 1), :] * acc_ref[0, :, sl]
               + gt_ref[pl.ds(base + 1, 1), :] * (acc_ref[1, :, sl] * inv_s[:, sl])
               + gt_ref[pl.ds(base + 2, 1), :] * acc_ref[2, :, sl])
        o_ref[0, :, HEAD_DIM * r:HEAD_DIM * (r + 1)] = o_t.T


def _attention(q_r, k_cmp, v_cmp_t, k_slc, v_slc_t, k_win, v_win_t, proj, b, s):
    g = N_KV_GROUPS
    nqb = s // Q_BLOCK
    n_cmp = s // CMP_STRIDE
    n_blk = s // SLC_BLOCK
    once = pl.Buffered(1)
    kspec = lambda n: pl.BlockSpec((1, 1, n, HEAD_DIM), lambda i, j, k: (i, j, 0, 0), pipeline_mode=once)
    vspec = lambda n: pl.BlockSpec((1, 1, HEAD_DIM, n), lambda i, j, k: (i, j, 0, 0), pipeline_mode=once)
    return pl.pallas_call(
        _attn_kernel,
        grid=(b, g, nqb),
        in_specs=[pl.BlockSpec((1, Q_BLOCK, GROUP_WIDTH), lambda i, j, k: (i, k, j)),
                  kspec(n_cmp), vspec(n_cmp), kspec(s), vspec(s), kspec(s), vspec(s),
                  pl.BlockSpec((Q_BLOCK, 128), lambda i, j, k: (i * nqb + k, GD_BLOCK))],
        out_specs=pl.BlockSpec((1, Q_BLOCK, GROUP_WIDTH), lambda i, j, k: (i, k, j)),
        out_shape=jax.ShapeDtypeStruct((b, s, D_MODEL), F32),
        scratch_shapes=[pltpu.VMEM((n_cmp, GROUP_WIDTH), F32),
                        pltpu.VMEM((n_cmp, Q_BLOCK), F32),
                        pltpu.VMEM((n_blk, Q_BLOCK), F32),
                        pltpu.VMEM((3, HEAD_DIM, GROUP_WIDTH), F32),
                        pltpu.VMEM((128, Q_BLOCK), F32)],
        compiler_params=_params(("parallel", "parallel", "arbitrary")),
        name="nsa_attention",
    )(q_r, k_cmp, v_cmp_t, k_slc, v_slc_t, k_win, v_win_t, proj)


def _ssm_kernel(xbc_ref, gd_ref, z_ref, cw_ref, cb_ref, dtb_ref, alog_ref, dsk_ref, nw_ref, tri_ref,
                y_ref, tail_ref, st_ref, u_ref):
    L = SSD_CHUNK

    @pl.when(pl.program_id(1) == 0)
    def _():
        tail_ref[...] = jnp.zeros_like(tail_ref)
        st_ref[...] = jnp.zeros_like(st_ref)

    strip = 512
    for cs in range(CONV_DIM // strip):
        sl = slice(strip * cs, strip * (cs + 1))
        cur = xbc_ref[:, sl]
        ext = jnp.concatenate([tail_ref[:, sl], cur], axis=0)
        acc = cb_ref[:, sl] + cw_ref[0:1, sl] * ext[5:5 + L]
        for k in range(1, CONV_WIDTH):
            acc = acc + cw_ref[k:k + 1, sl] * ext[5 + k:5 + k + L]
        tail_ref[:, sl] = cur[L - 8:L]
        u_ref[:, sl] = acc * _sigmoid(acc)

    x_dt = gd_ref[...] + dtb_ref[...]
    dt = jnp.maximum(x_dt, 0.0) + jnp.log1p(jnp.exp(-jnp.abs(x_dt)))
    da = dt * -jnp.exp(alog_ref[...])
    tri = tri_ref[...]
    cum = jnp.zeros((L, 128), F32)
    for part in _split3(da):
        cum = cum + _dot(tri, part)
    cum_t = cum.T
    row = lax.broadcasted_iota(jnp.int32, (L, L), 0)
    col = lax.broadcasted_iota(jnp.int32, (L, L), 1)
    causal = row >= col

    gw = SSM_HEADS_PER_GROUP * SSM_HEAD_DIM
    for g in range(N_SSM_GROUPS):
        b_g = u_ref[:, D_SSM + D_STATE * g:D_SSM + D_STATE * (g + 1)]
        c_g = u_ref[:, D_SSM + N_SSM_GROUPS * D_STATE + D_STATE * g:
                    D_SSM + N_SSM_GROUPS * D_STATE + D_STATE * (g + 1)]
        b_b = b_g.astype(BF16)
        c_b = c_g.astype(BF16)
        cb = _dot_nt(c_b, b_b)
        b_t = b_g.T.astype(BF16)
        st = st_ref[g]
        y_off = _dot(c_b, st.astype(BF16))
        ys, xends, decs = [], [], []
        for r in range(SSM_HEADS_PER_GROUP):
            h = SSM_HEADS_PER_GROUP * g + r
            lane = DT_LANE0 + h
            cc = cum[:, lane:lane + 1]
            cr = cum_t[lane:lane + 1, :]
            decay = jnp.exp(jnp.where(causal, cc - cr, -jnp.inf))
            w = (cb * decay).astype(BF16)
            xs_h = u_ref[:, SSM_HEAD_DIM * h:SSM_HEAD_DIM * (h + 1)]
            xdt = xs_h * dt[:, lane:lane + 1]
            y = (_dot(w, xdt.astype(BF16))
                 + y_off[:, SSM_HEAD_DIM * r:SSM_HEAD_DIM * (r + 1)] * jnp.exp(cc)
                 + dsk_ref[:, lane:lane + 1] * xs_h)
            last = cr[:, L - 1:L]
            xends.append(jnp.exp(last - cc) * xdt)
            decs.append(jnp.broadcast_to(jnp.exp(last), (1, SSM_HEAD_DIM)))
            ys.append(y)
        x_end = jnp.concatenate(xends, axis=1)
        st_ref[g] = st * jnp.concatenate(decs, axis=1) + _dot(b_t, x_end.astype(BF16))
        y_g = jnp.concatenate(ys, axis=1)
        sl = slice(gw * g, gw * (g + 1))
        z_g = z_ref[:, sl]
        v = y_g * (z_g * _sigmoid(z_g))
        ms = jnp.mean(v * v, axis=-1, keepdims=True)
        y_ref[:, sl] = (v * lax.rsqrt(ms + EPS) * nw_ref[:, sl]).astype(BF16)


def _ssm(proj, conv_w, conv_b, dtb, alog, dsk, norm_w, tri, b, s):
    nc = s // SSD_CHUNK
    full = lambda shape: pl.BlockSpec(shape, lambda i, j: (0,) * len(shape))
    return pl.pallas_call(
        _ssm_kernel,
        grid=(b, nc),
        in_specs=[pl.BlockSpec((SSD_CHUNK, CONV_DIM), lambda i, j: (i * nc + j, 1)),
                  pl.BlockSpec((SSD_CHUNK, 128), lambda i, j: (i * nc + j, GD_BLOCK)),
                  pl.BlockSpec((SSD_CHUNK, D_SSM), lambda i, j: (i * nc + j, 5)),
                  full((CONV_WIDTH, CONV_DIM)), full((1, CONV_DIM)),
                  full((1, 128)), full((1, 128)), full((1, 128)),
                  full((1, D_SSM)), full((SSD_CHUNK, SSD_CHUNK))],
        out_specs=pl.BlockSpec((SSD_CHUNK, D_SSM), lambda i, j: (i * nc + j, 0)),
        out_shape=jax.ShapeDtypeStruct((b * s, D_SSM), BF16),
        scratch_shapes=[pltpu.VMEM((8, CONV_DIM), F32),
                        pltpu.VMEM((N_SSM_GROUPS, D_STATE, SSM_HEADS_PER_GROUP * SSM_HEAD_DIM), F32),
                        pltpu.VMEM((SSD_CHUNK, CONV_DIM), F32)],
        compiler_params=_params(("parallel", "arbitrary")),
        name="ssd_scan",
    )(proj, proj, proj, conv_w, conv_b, dtb, alog, dsk, norm_w, tri)


def _out_kernel(o_ref, z_ref, ys_ref, x_ref, naw_ref, w_ref, fw_ref, out_ref, *, final):
    z = z_ref[...]
    v = o_ref[...] * (z * _sigmoid(z))
    ms = jnp.mean(v * v, axis=-1, keepdims=True)
    ya = (v * lax.rsqrt(ms + EPS) * naw_ref[...]).astype(BF16)
    acc = x_ref[...] + _dot(ya, w_ref[0:D_MODEL, :]) + _dot(ys_ref[...], w_ref[D_MODEL:2 * D_MODEL, :])
    if final:
        ms2 = jnp.mean(acc * acc, axis=-1, keepdims=True)
        acc = acc * lax.rsqrt(ms2 + EPS) * fw_ref[...]
    out_ref[...] = acc


def _out_proj(o_attn, proj, y_ssm, x2d, norm_attn_w, w_out, final_w, final):
    t = x2d.shape[0]
    rowblk = lambda c: pl.BlockSpec((OUT_TM, D_MODEL), lambda i: (i, c))
    vec = pl.BlockSpec((1, D_MODEL), lambda i: (0, 0))
    return pl.pallas_call(
        functools.partial(_out_kernel, final=final),
        grid=(t // OUT_TM,),
        in_specs=[rowblk(0), rowblk(4), rowblk(0), rowblk(0), vec,
                  pl.BlockSpec((2 * D_MODEL, D_MODEL), lambda i: (0, 0), pipeline_mode=pl.Buffered(1)),
                  vec],
        out_specs=rowblk(0),
        out_shape=jax.ShapeDtypeStruct((t, D_MODEL), F32),
        compiler_params=_params(("parallel",)),
        name="out_proj",
    )(o_attn, proj, y_ssm, x2d, norm_attn_w, w_out, final_w)


def _pack_w_in(w):
    q0, kv0, gate0, za0, xbc0, dt0, zs0, end = 0, 2048, 3584, 3632, 5680, 9776, 9808, 11856
    pad = jnp.zeros((w.shape[0], 2048 - KV_COLS - GATE_LANES - N_HEADS_SSM), w.dtype)
    return jnp.concatenate(
        [w[:, kv0:gate0], w[:, gate0:za0], w[:, dt0:zs0], pad,
         w[:, q0:kv0], w[:, xbc0:dt0], w[:, za0:xbc0], w[:, zs0:end]], axis=1).astype(BF16)


def _dt_lanes(v):
    return jnp.zeros((1, 128), F32).at[0, DT_LANE0:DT_LANE0 + N_HEADS_SSM].set(v.astype(F32))


def kernel(x, positions, norm_w, w_in, cmp_pe_k, cmp_w1_k, cmp_w2_k, cmp_pe_v, cmp_w1_v, cmp_w2_v,
           conv_w, conv_b, dt_bias, a_log, d_skip, norm_attn_w, norm_ssm_w, w_out, final_norm_w):
    b, s, d = x.shape
    depth = w_in.shape[0]
    assert d == D_MODEL and s % 2048 == 0
    n_chunks = s // CMP_STRIDE

    half = HEAD_DIM // 2
    inv_freq = ROPE_THETA ** (-jnp.arange(half, dtype=F32) / half)
    inv_full = jnp.concatenate([inv_freq, inv_freq])[None, :]
    pos_f = positions.astype(F32)
    cos_t, sin_t = _rope_tables(pos_f[:, :, None], inv_full)
    pos_c = jnp.concatenate([pos_f[:, CMP_BLOCK - 1::CMP_STRIDE], pos_f[:, -1:]], axis=1)
    cos_c, sin_c = _rope_tables(pos_c[:, :, None], inv_full)

    tri = jnp.tril(jnp.ones((SSD_CHUNK, SSD_CHUNK), F32)).astype(BF16)
    x2d = x.reshape(b * s, d)
    for l in range(depth):
        w_pack = _pack_w_in(w_in[l])
        w1 = jnp.stack([cmp_w1_k[l], cmp_w1_v[l]]).reshape(2, 2, CMP_STRIDE * HEAD_DIM, CMP_HIDDEN).astype(BF16)
        w2 = jnp.stack([cmp_w2_k[l], cmp_w2_v[l]]).astype(BF16)
        pe = jnp.stack([cmp_pe_k[l], cmp_pe_v[l]]).reshape(2, 2, 1, CMP_STRIDE * HEAD_DIM)

        proj = _in_proj(x2d, norm_w[l][None, :], w_pack)
        q_r, kcvc, k_slc, k_win, v_slc_t, v_win_t = _prep(proj, cos_t, sin_t, b, s)
        kcvc = kcvc.reshape(b, 2, N_KV_GROUPS, n_chunks, CMP_STRIDE * HEAD_DIM)
        k_cmp, v_cmp_t = _compress(kcvc, w1, w2, pe, cos_c, sin_c)
        o_attn = _attention(q_r, k_cmp, v_cmp_t, k_slc, v_slc_t, k_win, v_win_t, proj, b, s)
        y_ssm = _ssm(proj, conv_w[l], conv_b[l][None, :], _dt_lanes(dt_bias[l]),
                     _dt_lanes(a_log[l]), _dt_lanes(d_skip[l]),
                     norm_ssm_w[l][None, :], tri, b, s)
        x2d = _out_proj(o_attn.reshape(b * s, d), proj, y_ssm, x2d, norm_attn_w[l][None, :],
                        w_out[l].astype(BF16), final_norm_w[None, :], final=(l == depth - 1))
    return x2d.reshape(b, s, d)
```

```python
import functools

import jax
import jax.numpy as jnp
from jax import lax
from jax.experimental import pallas as pl
from jax.experimental.pallas import tpu as pltpu

F32 = jnp.float32
BF16 = jnp.bfloat16

D_MODEL = 2048
HEAD_DIM = 128
N_HEADS_ATTN = 16
N_KV_GROUPS = 2
HEADS_PER_GROUP = 8
GROUP_WIDTH = HEADS_PER_GROUP * HEAD_DIM
CMP_STRIDE = 16
CMP_BLOCK = 32
CMP_HIDDEN = 256
SLC_BLOCK = 64
N_SELECT = 16
WINDOW = 512
Q_BLOCK = 128
ROPE_THETA = 10000.0
NEG_INF = -1e30
SEL_FORCED = 1e4
N_HEADS_SSM = 32
SSM_HEAD_DIM = 64
N_SSM_GROUPS = 8
SSM_HEADS_PER_GROUP = 4
D_STATE = 128
D_SSM = 2048
CONV_WIDTH = 4
CONV_DIM = 4096
SSD_CHUNK = 256
EPS = 1e-6
Q_SCALE = HEAD_DIM ** -0.5 * 1.4426950408889634

KV_COLS = 6 * N_KV_GROUPS * HEAD_DIM
GD_BLOCK = KV_COLS // 128
GATE_LANES = 3 * N_HEADS_ATTN
DT_LANE0 = GATE_LANES
D_PACK = 12288
PROJ_TM = 512
PROJ_TN = 2048
PREP_TQ = 256
OUT_TM = 256
CMP_CHUNK = 128
SEL_CHUNK = 256
WIN_SPAN = WINDOW + Q_BLOCK
VMEM_LIMIT = 56 * 1024 * 1024


def _dot(a, b):
    return jnp.dot(a, b, preferred_element_type=F32)


def _dot_nt(a, b):
    return lax.dot_general(a, b, (((1,), (1,)), ((), ())), preferred_element_type=F32)


def _split3(x):
    hi = x.astype(BF16)
    r1 = x - hi.astype(F32)
    mid = r1.astype(BF16)
    lo = (r1 - mid.astype(F32)).astype(BF16)
    return hi, mid, lo


def _sigmoid(x):
    return 1.0 / (1.0 + jnp.exp(-x))


def _params(sem):
    return pltpu.CompilerParams(dimension_semantics=sem, vmem_limit_bytes=VMEM_LIMIT)


def _rope_table_kernel(pos_ref, inv_ref, cos_ref, sin_ref):
    ang = pos_ref[0] * inv_ref[...]
    lane = lax.broadcasted_iota(jnp.int32, ang.shape, 1)
    cos_ref[0] = jnp.cos(ang)
    s = jnp.sin(ang)
    sin_ref[0] = jnp.where(lane < HEAD_DIM // 2, -s, s)


def _rope_tables(pos_f, inv_full):
    b, n, _ = pos_f.shape
    rows = n if n <= 2048 else 2048
    out = jax.ShapeDtypeStruct((b, n, HEAD_DIM), F32)
    return pl.pallas_call(
        _rope_table_kernel,
        grid=(b, n // rows),
        in_specs=[pl.BlockSpec((1, rows, 1), lambda i, j: (i, j, 0)),
                  pl.BlockSpec((1, HEAD_DIM), lambda i, j: (0, 0))],
        out_specs=[pl.BlockSpec((1, rows, HEAD_DIM), lambda i, j: (i, j, 0))] * 2,
        out_shape=[out, out],
        compiler_params=_params(("parallel", "parallel")),
        name="rope_tables",
    )(pos_f, inv_full)


def _proj_kernel(x_ref, nw_ref, w_ref, o_ref, h_ref):
    @pl.when(pl.program_id(1) == 0)
    def _():
        x = x_ref[...]
        ms = jnp.mean(x * x, axis=-1, keepdims=True)
        h_ref[...] = (x * lax.rsqrt(ms + EPS) * nw_ref[...]).astype(BF16)

    o_ref[...] = _dot(h_ref[...], w_ref[...])


def _in_proj(x2d, norm_w, w_pack):
    t = x2d.shape[0]
    return pl.pallas_call(
        _proj_kernel,
        grid=(t // PROJ_TM, D_PACK // PROJ_TN),
        in_specs=[pl.BlockSpec((PROJ_TM, D_MODEL), lambda i, j: (i, 0)),
                  pl.BlockSpec((1, D_MODEL), lambda i, j: (0, 0)),
                  pl.BlockSpec((D_MODEL, PROJ_TN), lambda i, j: (0, j))],
        out_specs=pl.BlockSpec((PROJ_TM, PROJ_TN), lambda i, j: (i, j)),
        out_shape=jax.ShapeDtypeStruct((t, D_PACK), F32),
        scratch_shapes=[pltpu.VMEM((PROJ_TM, D_MODEL), BF16)],
        compiler_params=_params(("parallel", "arbitrary")),
        name="in_proj",
    )(x2d, norm_w, w_pack)


def _prep_kernel(kv_ref, q_ref, cos_ref, sin_ref,
                 qo_ref, kcvc_ref, ks_ref, kw_ref, vst_ref, vwt_ref):
    c = cos_ref[0]
    s = sin_ref[0]

    def rope(u):
        return u * c + pltpu.roll(u, HEAD_DIM // 2, 1) * s

    for h in range(N_HEADS_ATTN):
        sl = slice(HEAD_DIM * h, HEAD_DIM * (h + 1))
        qo_ref[0, :, sl] = (rope(q_ref[:, sl]) * Q_SCALE).astype(BF16)

    def seg(i, g):
        lo = (i * N_KV_GROUPS + g) * HEAD_DIM
        return kv_ref[:, lo:lo + HEAD_DIM]

    for g in range(N_KV_GROUPS):
        kcvc_ref[0, 0, g] = seg(0, g).astype(BF16)
        kcvc_ref[0, 1, g] = seg(1, g).astype(BF16)
        ks_ref[0, g] = rope(seg(2, g)).astype(BF16)
        vst_ref[0, g] = seg(3, g).T.astype(BF16)
        kw_ref[0, g] = rope(seg(4, g)).astype(BF16)
        vwt_ref[0, g] = seg(5, g).T.astype(BF16)


def _prep(proj, cos_t, sin_t, b, s):
    nq = s // PREP_TQ
    g = N_KV_GROUPS
    row = lambda i, j: (i * nq + j, 0)
    kshape = jax.ShapeDtypeStruct((b, g, s, HEAD_DIM), BF16)
    vshape = jax.ShapeDtypeStruct((b, g, HEAD_DIM, s), BF16)
    kspec = pl.BlockSpec((1, g, PREP_TQ, HEAD_DIM), lambda i, j: (i, 0, j, 0))
    vspec = pl.BlockSpec((1, g, HEAD_DIM, PREP_TQ), lambda i, j: (i, 0, 0, j))
    return pl.pallas_call(
        _prep_kernel,
        grid=(b, nq),
        in_specs=[pl.BlockSpec((PREP_TQ, KV_COLS), row),
                  pl.BlockSpec((PREP_TQ, D_MODEL), lambda i, j: (i * nq + j, 1)),
                  pl.BlockSpec((1, PREP_TQ, HEAD_DIM), lambda i, j: (i, j, 0)),
                  pl.BlockSpec((1, PREP_TQ, HEAD_DIM), lambda i, j: (i, j, 0))],
        out_specs=[pl.BlockSpec((1, PREP_TQ, D_MODEL), lambda i, j: (i, j, 0)),
                   pl.BlockSpec((1, 2, g, PREP_TQ, HEAD_DIM), lambda i, j: (i, 0, 0, j, 0)),
                   kspec, kspec, vspec, vspec],
        out_shape=[jax.ShapeDtypeStruct((b, s, D_MODEL), BF16),
                   jax.ShapeDtypeStruct((b, 2, g, s, HEAD_DIM), BF16),
                   kshape, kshape, vshape, vshape],
        compiler_params=_params(("parallel", "parallel")),
        name="rope_relayout",
    )(proj, proj, cos_t, sin_t)


def _cmp_kernel(x_ref, w1_ref, w2_ref, pe_ref, cos_ref, sin_ref, kc_ref, vct_ref):
    n = x_ref.shape[3]
    for kv in range(2):
        x = x_ref[0, kv, 0]
        a0 = _dot(x, w1_ref[kv, 0])
        a1 = _dot(x, w1_ref[kv, 1])
        bias = jnp.zeros((8, CMP_HIDDEN), F32)
        for r in range(2):
            pe = jnp.broadcast_to(pe_ref[kv, r], (8, CMP_STRIDE * HEAD_DIM))
            for part in _split3(pe):
                bias = bias + _dot(part, w1_ref[kv, r])
        hid = a0 + pltpu.roll(a1, n - 1, 0) + bias[0:1]
        act = hid * _sigmoid(hid)
        out = _dot(act.astype(BF16), w2_ref[kv])
        if kv == 0:
            roped = out * cos_ref[0] + pltpu.roll(out, HEAD_DIM // 2, 1) * sin_ref[0]
            kc_ref[0, 0] = roped.astype(BF16)
        else:
            vct_ref[0, 0] = out.T.astype(BF16)


def _compress(kcvc, w1, w2, pe, cos_c, sin_c):
    b, _, g, n, width = kcvc.shape
    return pl.pallas_call(
        _cmp_kernel,
        grid=(b, g),
        in_specs=[pl.BlockSpec((1, 2, 1, n, width), lambda i, j: (i, 0, j, 0, 0)),
                  pl.BlockSpec(w1.shape, lambda i, j: (0, 0, 0, 0)),
                  pl.BlockSpec(w2.shape, lambda i, j: (0, 0, 0)),
                  pl.BlockSpec(pe.shape, lambda i, j: (0, 0, 0, 0)),
                  pl.BlockSpec((1, n, HEAD_DIM), lambda i, j: (i, 0, 0)),
                  pl.BlockSpec((1, n, HEAD_DIM), lambda i, j: (i, 0, 0))],
        out_specs=[pl.BlockSpec((1, 1, n, HEAD_DIM), lambda i, j: (i, j, 0, 0)),
                   pl.BlockSpec((1, 1, HEAD_DIM, n), lambda i, j: (i, j, 0, 0))],
        out_shape=[jax.ShapeDtypeStruct((b, g, n, HEAD_DIM), BF16),
                   jax.ShapeDtypeStruct((b, g, HEAD_DIM, n), BF16)],
        compiler_params=_params(("parallel", "parallel")),
        name="compress",
    )(kcvc, w1, w2, pe, cos_c, sin_c)


def _attn_kernel(q_ref, kc_ref, vct_ref, ks_ref, vst_ref, kw_ref, vwt_ref, gd_ref, o_ref,
                 sc_ref, imp_ref, sel_ref, acc_ref, gt_ref, sa_ref, sb_ref):
    n_cmp = kc_ref.shape[2]
    n_blk = sel_ref.shape[0]
    g = pl.program_id(1)
    qi = pl.program_id(2)
    t0 = qi * Q_BLOCK
    lanes_all = GROUP_WIDTH

    qs = jnp.concatenate(
        [q_ref[0, :, HEAD_DIM * r:HEAD_DIM * (r + 1)] for r in range(HEADS_PER_GROUP)], axis=0)
    t_lane = t0 + lax.broadcasted_iota(jnp.int32, (1, Q_BLOCK), 1)

    def tile_heads(x, n):
        return jnp.concatenate([x] * n, axis=1)

    n_c = (8 * qi + 6) // CMP_CHUNK + 1
    imp_ref[...] = jnp.zeros_like(imp_ref)
    acc_ref[...] = jnp.zeros_like(acc_ref)

    def cmp_scores(c, m):
        off = pl.multiple_of(c * CMP_CHUNK, CMP_CHUNK)
        s = _dot_nt(kc_ref[0, 0, pl.ds(off, CMP_CHUNK), :], qs)
        ci = off + lax.broadcasted_iota(jnp.int32, (CMP_CHUNK, Q_BLOCK), 0)
        vis = ci * CMP_STRIDE + (CMP_BLOCK - 1) <= t_lane
        s = s + tile_heads(jnp.where(vis, 0.0, NEG_INF), HEADS_PER_GROUP)
        sc_ref[pl.ds(off, CMP_CHUNK), :] = s
        return jnp.maximum(m, jnp.max(s, axis=0, keepdims=True))

    m_c = lax.fori_loop(0, n_c, cmp_scores, jnp.full((1, lanes_all), NEG_INF, F32))

    def cmp_exp(c, l):
        off = pl.multiple_of(c * CMP_CHUNK, CMP_CHUNK)
        s = sc_ref[pl.ds(off, CMP_CHUNK), :]
        e = jnp.where(s > 0.5 * NEG_INF, jnp.exp2(s - m_c), 0.0)
        sc_ref[pl.ds(off, CMP_CHUNK), :] = e
        return l + jnp.sum(e, axis=0, keepdims=True)

    l_c = lax.fori_loop(0, n_c, cmp_exp, jnp.zeros((1, lanes_all), F32))
    inv_c = 1.0 / jnp.where(l_c > 0.0, l_c, 1.0)

    def cmp_out(c, carry):
        off = pl.multiple_of(c * CMP_CHUNK, CMP_CHUNK)
        pn = sc_ref[pl.ds(off, CMP_CHUNK), :] * inv_c
        imp = pn[:, 0:Q_BLOCK]
        for r in range(1, HEADS_PER_GROUP):
            imp = imp + pn[:, Q_BLOCK * r:Q_BLOCK * (r + 1)]
        imp_ref[pl.ds(off, CMP_CHUNK), :] = imp
        acc_ref[0] += _dot(vct_ref[0, 0, :, pl.ds(off, CMP_CHUNK)], pn.astype(BF16))
        return carry

    lax.fori_loop(0, n_c, cmp_out, 0)

    ratio = SLC_BLOCK // CMP_STRIDE
    score_in = pltpu.roll(imp_ref[pl.ds(ratio - 1, n_blk, stride=ratio), :], 1, 0)
    for m in range(ratio):
        score_in = score_in + imp_ref[pl.ds(m, n_blk, stride=ratio), :]
    blk = lax.broadcasted_iota(jnp.int32, (n_blk, Q_BLOCK), 0)
    cur = t_lane // SLC_BLOCK
    forced = (blk == 0) | (blk == cur) | (blk == cur - 1)
    visible = blk <= cur
    score0 = jnp.where(forced, SEL_FORCED, jnp.where(visible, score_in, -SEL_FORCED))
    blk_f = blk.astype(F32)

    def pick(_, carry):
        score, sel = carry
        mx = jnp.max(score, axis=0, keepdims=True)
        first = jnp.min(jnp.where(score == mx, blk_f, float(n_blk)), axis=0, keepdims=True)
        hit = blk_f == first
        return jnp.where(hit, -jnp.inf, score), jnp.where(hit, 1.0, sel)

    _, sel = lax.fori_loop(0, N_SELECT, pick, (score0, jnp.zeros((n_blk, Q_BLOCK), F32)))
    sel_ref[...] = jnp.where(visible, sel, 0.0)

    n_k = (t0 + Q_BLOCK - 1) // (2 * SEL_CHUNK) + 1
    blocks_per_chunk = SEL_CHUNK // SLC_BLOCK
    ones_rows = jnp.ones((16, SEL_CHUNK), BF16)

    def score_tile(off, s_ref):
        rows = [jnp.broadcast_to(sel_ref[pl.ds(off // SLC_BLOCK + j, 1), :], (SLC_BLOCK, Q_BLOCK))
                for j in range(blocks_per_chunk)]
        picked = jnp.concatenate(rows, axis=0)
        kpos = off + lax.broadcasted_iota(jnp.int32, (SEL_CHUNK, Q_BLOCK), 0)
        bias = jnp.where((picked > 0.5) & (kpos <= t_lane), 0.0, NEG_INF)
        s = _dot_nt(ks_ref[0, 0, pl.ds(off, SEL_CHUNK), :], qs) + tile_heads(bias, HEADS_PER_GROUP)
        s_ref[...] = s
        return jnp.max(s, axis=0, keepdims=True)

    def value_tile(off, s_ref, tile_max, m, l):
        vt = jnp.concatenate([vst_ref[0, 0, :, pl.ds(off, SEL_CHUNK)], ones_rows], axis=0)
        m_new = jnp.maximum(m, tile_max)
        alpha = jnp.exp2(m - m_new)
        pv = _dot(vt, jnp.exp2(s_ref[...] - m_new).astype(BF16))
        acc_ref[1] = alpha * acc_ref[1] + pv[0:HEAD_DIM]
        return m_new, alpha * l + pv[HEAD_DIM:HEAD_DIM + 1]

    def sel_step(c, carry):
        m, l, max_a = carry
        off_a = pl.multiple_of(c * (2 * SEL_CHUNK), 2 * SEL_CHUNK)
        off_b = pl.multiple_of(off_a + SEL_CHUNK, SEL_CHUNK)
        max_b = score_tile(off_b, sb_ref)
        m, l = value_tile(off_a, sa_ref, max_a, m, l)
        next_a = pl.multiple_of(jnp.minimum(c + 1, n_k - 1) * (2 * SEL_CHUNK), 2 * SEL_CHUNK)
        max_a = score_tile(next_a, sa_ref)
        m, l = value_tile(off_b, sb_ref, max_b, m, l)
        return m, l, max_a

    max_a0 = score_tile(0, sa_ref)
    _, l_s, _ = lax.fori_loop(0, n_k, sel_step,
                              (jnp.full((1, lanes_all), NEG_INF, F32), jnp.zeros((1, lanes_all), F32), max_a0))
    inv_s = 1.0 / l_s
    pair = 2 * Q_BLOCK

    ws = pl.multiple_of(jnp.maximum(t0 - WINDOW, 0), Q_BLOCK)
    kw = kw_ref[0, 0, pl.ds(ws, WIN_SPAN), :]
    vwt = vwt_ref[0, 0, :, pl.ds(ws, WIN_SPAN)]
    kpos = ws + lax.broadcasted_iota(jnp.int32, (WIN_SPAN, Q_BLOCK), 0)
    in_win = (kpos <= t_lane) & (kpos > t_lane - WINDOW)
    wbias = tile_heads(jnp.where(in_win, 0.0, NEG_INF), 2)
    for hp in range(HEADS_PER_GROUP // 2):
        sl = slice(pair * hp, pair * (hp + 1))
        s = _dot_nt(kw, qs[sl]) + wbias
        p = jnp.exp2(s - jnp.max(s, axis=0, keepdims=True))
        inv_w = 1.0 / jnp.sum(p, axis=0, keepdims=True)
        acc_ref[2, :, sl] = _dot(vwt, p.astype(BF16)) * inv_w

    gt_ref[...] = _sigmoid(gd_ref[...]).T
    for r in range(HEADS_PER_GROUP):
        sl = slice(Q_BLOCK * r, Q_BLOCK * (r + 1))
        base = 3 * (HEADS_PER_GROUP * g + r)
        o_t = (gt_ref[pl.ds(base, 1), :] * acc_ref[0, :, sl]
               + gt_ref[pl.ds(base + 1, 1), :] * (acc_ref[1, :, sl] * inv_s[:, sl])
               + gt_ref[pl.ds(base + 2, 1), :] * acc_ref[2, :, sl])
        o_ref[0, :, HEAD_DIM * r:HEAD_DIM * (r + 1)] = o_t.T


def _attention(q_r, k_cmp, v_cmp_t, k_slc, v_slc_t, k_win, v_win_t, proj, b, s):
    g = N_KV_GROUPS
    nqb = s // Q_BLOCK
    n_cmp = s // CMP_STRIDE
    n_blk = s // SLC_BLOCK
    once = pl.Buffered(1)
    kspec = lambda n: pl.BlockSpec((1, 1, n, HEAD_DIM), lambda i, j, k: (i, j, 0, 0), pipeline_mode=once)
    vspec = lambda n: pl.BlockSpec((1, 1, HEAD_DIM, n), lambda i, j, k: (i, j, 0, 0), pipeline_mode=once)
    return pl.pallas_call(
        _attn_kernel,
        grid=(b, g, nqb),
        in_specs=[pl.BlockSpec((1, Q_BLOCK, GROUP_WIDTH), lambda i, j, k: (i, k, j)),
                  kspec(n_cmp), vspec(n_cmp), kspec(s), vspec(s), kspec(s), vspec(s),
                  pl.BlockSpec((Q_BLOCK, 128), lambda i, j, k: (i * nqb + k, GD_BLOCK))],
        out_specs=pl.BlockSpec((1, Q_BLOCK, GROUP_WIDTH), lambda i, j, k: (i, k, j)),
        out_shape=jax.ShapeDtypeStruct((b, s, D_MODEL), F32),
        scratch_shapes=[pltpu.VMEM((n_cmp, GROUP_WIDTH), F32),
                        pltpu.VMEM((n_cmp, Q_BLOCK), F32),
                        pltpu.VMEM((n_blk, Q_BLOCK), F32),
                        pltpu.VMEM((3, HEAD_DIM, GROUP_WIDTH), F32),
                        pltpu.VMEM((128, Q_BLOCK), F32),
                        pltpu.VMEM((SEL_CHUNK, GROUP_WIDTH), F32),
                        pltpu.VMEM((SEL_CHUNK, GROUP_WIDTH), F32)],
        compiler_params=_params(("parallel", "parallel", "arbitrary")),
        name="nsa_attention",
    )(q_r, k_cmp, v_cmp_t, k_slc, v_slc_t, k_win, v_win_t, proj)


def _ssm_kernel(xbc_ref, gd_ref, z_ref, cw_ref, cb_ref, dtb_ref, alog_ref, dsk_ref, nw_ref, tri_ref,
                y_ref, tail_ref, st_ref, u_ref):
    L = SSD_CHUNK

    @pl.when(pl.program_id(1) == 0)
    def _():
        tail_ref[...] = jnp.zeros_like(tail_ref)
        st_ref[...] = jnp.zeros_like(st_ref)

    strip = 512
    for cs in range(CONV_DIM // strip):
        sl = slice(strip * cs, strip * (cs + 1))
        cur = xbc_ref[:, sl]
        ext = jnp.concatenate([tail_ref[:, sl], cur], axis=0)
        acc = cb_ref[:, sl] + cw_ref[0:1, sl] * ext[5:5 + L]
        for k in range(1, CONV_WIDTH):
            acc = acc + cw_ref[k:k + 1, sl] * ext[5 + k:5 + k + L]
        tail_ref[:, sl] = cur[L - 8:L]
        u_ref[:, sl] = acc * _sigmoid(acc)

    x_dt = gd_ref[...] + dtb_ref[...]
    dt = jnp.maximum(x_dt, 0.0) + jnp.log1p(jnp.exp(-jnp.abs(x_dt)))
    da = dt * -jnp.exp(alog_ref[...])
    tri = tri_ref[...]
    cum = jnp.zeros((L, 128), F32)
    for part in _split3(da):
        cum = cum + _dot(tri, part)
    cum_t = cum.T
    row = lax.broadcasted_iota(jnp.int32, (L, L), 0)
    col = lax.broadcasted_iota(jnp.int32, (L, L), 1)
    causal = row >= col

    gw = SSM_HEADS_PER_GROUP * SSM_HEAD_DIM
    for g in range(N_SSM_GROUPS):
        b_g = u_ref[:, D_SSM + D_STATE * g:D_SSM + D_STATE * (g + 1)]
        c_g = u_ref[:, D_SSM + N_SSM_GROUPS * D_STATE + D_STATE * g:
                    D_SSM + N_SSM_GROUPS * D_STATE + D_STATE * (g + 1)]
        b_b = b_g.astype(BF16)
        c_b = c_g.astype(BF16)
        cb = _dot_nt(c_b, b_b)
        b_t = b_g.T.astype(BF16)
        st = st_ref[g]
        y_off = _dot(c_b, st.astype(BF16))
        ys, xends, decs = [], [], []
        for r in range(SSM_HEADS_PER_GROUP):
            h = SSM_HEADS_PER_GROUP * g + r
            lane = DT_LANE0 + h
            cc = cum[:, lane:lane + 1]
            cr = cum_t[lane:lane + 1, :]
            decay = jnp.exp(jnp.where(causal, cc - cr, -jnp.inf))
            w = (cb * decay).astype(BF16)
            xs_h = u_ref[:, SSM_HEAD_DIM * h:SSM_HEAD_DIM * (h + 1)]
            xdt = xs_h * dt[:, lane:lane + 1]
            y = (_dot(w, xdt.astype(BF16))
                 + y_off[:, SSM_HEAD_DIM * r:SSM_HEAD_DIM * (r + 1)] * jnp.exp(cc)
                 + dsk_ref[:, lane:lane + 1] * xs_h)
            last = cr[:, L - 1:L]
            xends.append(jnp.exp(last - cc) * xdt)
            decs.append(jnp.broadcast_to(jnp.exp(last), (1, SSM_HEAD_DIM)))
            ys.append(y)
        x_end = jnp.concatenate(xends, axis=1)
        st_ref[g] = st * jnp.concatenate(decs, axis=1) + _dot(b_t, x_end.astype(BF16))
        y_g = jnp.concatenate(ys, axis=1)
        sl = slice(gw * g, gw * (g + 1))
        z_g = z_ref[:, sl]
        v = y_g * (z_g * _sigmoid(z_g))
        ms = jnp.mean(v * v, axis=-1, keepdims=True)
        y_ref[:, sl] = (v * lax.rsqrt(ms + EPS) * nw_ref[:, sl]).astype(BF16)


def _ssm(proj, conv_w, conv_b, dtb, alog, dsk, norm_w, tri, b, s):
    nc = s // SSD_CHUNK
    full = lambda shape: pl.BlockSpec(shape, lambda i, j: (0,) * len(shape))
    return pl.pallas_call(
        _ssm_kernel,
        grid=(b, nc),
        in_specs=[pl.BlockSpec((SSD_CHUNK, CONV_DIM), lambda i, j: (i * nc + j, 1)),
                  pl.BlockSpec((SSD_CHUNK, 128), lambda i, j: (i * nc + j, GD_BLOCK)),
                  pl.BlockSpec((SSD_CHUNK, D_SSM), lambda i, j: (i * nc + j, 5)),
                  full((CONV_WIDTH, CONV_DIM)), full((1, CONV_DIM)),
                  full((1, 128)), full((1, 128)), full((1, 128)),
                  full((1, D_SSM)), full((SSD_CHUNK, SSD_CHUNK))],
        out_specs=pl.BlockSpec((SSD_CHUNK, D_SSM), lambda i, j: (i * nc + j, 0)),
        out_shape=jax.ShapeDtypeStruct((b * s, D_SSM), BF16),
        scratch_shapes=[pltpu.VMEM((8, CONV_DIM), F32),
                        pltpu.VMEM((N_SSM_GROUPS, D_STATE, SSM_HEADS_PER_GROUP * SSM_HEAD_DIM), F32),
                        pltpu.VMEM((SSD_CHUNK, CONV_DIM), F32)],
        compiler_params=_params(("parallel", "arbitrary")),
        name="ssd_scan",
    )(proj, proj, proj, conv_w, conv_b, dtb, alog, dsk, norm_w, tri)


def _out_kernel(o_ref, z_ref, ys_ref, x_ref, naw_ref, w_ref, fw_ref, out_ref, *, final):
    z = z_ref[...]
    v = o_ref[...] * (z * _sigmoid(z))
    ms = jnp.mean(v * v, axis=-1, keepdims=True)
    ya = (v * lax.rsqrt(ms + EPS) * naw_ref[...]).astype(BF16)
    acc = x_ref[...] + _dot(ya, w_ref[0:D_MODEL, :]) + _dot(ys_ref[...], w_ref[D_MODEL:2 * D_MODEL, :])
    if final:
        ms2 = jnp.mean(acc * acc, axis=-1, keepdims=True)
        acc = acc * lax.rsqrt(ms2 + EPS) * fw_ref[...]
    out_ref[...] = acc


def _out_proj(o_attn, proj, y_ssm, x2d, norm_attn_w, w_out, final_w, final):
    t = x2d.shape[0]
    rowblk = lambda c: pl.BlockSpec((OUT_TM, D_MODEL), lambda i: (i, c))
    vec = pl.BlockSpec((1, D_MODEL), lambda i: (0, 0))
    return pl.pallas_call(
        functools.partial(_out_kernel, final=final),
        grid=(t // OUT_TM,),
        in_specs=[rowblk(0), rowblk(4), rowblk(0), rowblk(0), vec,
                  pl.BlockSpec((2 * D_MODEL, D_MODEL), lambda i: (0, 0), pipeline_mode=pl.Buffered(1)),
                  vec],
        out_specs=rowblk(0),
        out_shape=jax.ShapeDtypeStruct((t, D_MODEL), F32),
        compiler_params=_params(("parallel",)),
        name="out_proj",
    )(o_attn, proj, y_ssm, x2d, norm_attn_w, w_out, final_w)


def _pack_w_in(w):
    q0, kv0, gate0, za0, xbc0, dt0, zs0, end = 0, 2048, 3584, 3632, 5680, 9776, 9808, 11856
    pad = jnp.zeros((w.shape[0], 2048 - KV_COLS - GATE_LANES - N_HEADS_SSM), w.dtype)
    return jnp.concatenate(
        [w[:, kv0:gate0], w[:, gate0:za0], w[:, dt0:zs0], pad,
         w[:, q0:kv0], w[:, xbc0:dt0], w[:, za0:xbc0], w[:, zs0:end]], axis=1).astype(BF16)


def _dt_lanes(v):
    return jnp.zeros((1, 128), F32).at[0, DT_LANE0:DT_LANE0 + N_HEADS_SSM].set(v.astype(F32))


def kernel(x, positions, norm_w, w_in, cmp_pe_k, cmp_w1_k, cmp_w2_k, cmp_pe_v, cmp_w1_v, cmp_w2_v,
           conv_w, conv_b, dt_bias, a_log, d_skip, norm_attn_w, norm_ssm_w, w_out, final_norm_w):
    b, s, d = x.shape
    depth = w_in.shape[0]
    assert d == D_MODEL and s % 2048 == 0
    n_chunks = s // CMP_STRIDE

    half = HEAD_DIM // 2
    inv_freq = ROPE_THETA ** (-jnp.arange(half, dtype=F32) / half)
    inv_full = jnp.concatenate([inv_freq, inv_freq])[None, :]
    pos_f = positions.astype(F32)
    cos_t, sin_t = _rope_tables(pos_f[:, :, None], inv_full)
    pos_c = jnp.concatenate([pos_f[:, CMP_BLOCK - 1::CMP_STRIDE], pos_f[:, -1:]], axis=1)
    cos_c, sin_c = _rope_tables(pos_c[:, :, None], inv_full)

    tri = jnp.tril(jnp.ones((SSD_CHUNK, SSD_CHUNK), F32)).astype(BF16)
    x2d = x.reshape(b * s, d)
    for l in range(depth):
        w_pack = _pack_w_in(w_in[l])
        w1 = jnp.stack([cmp_w1_k[l], cmp_w1_v[l]]).reshape(2, 2, CMP_STRIDE * HEAD_DIM, CMP_HIDDEN).astype(BF16)
        w2 = jnp.stack([cmp_w2_k[l], cmp_w2_v[l]]).astype(BF16)
        pe = jnp.stack([cmp_pe_k[l], cmp_pe_v[l]]).reshape(2, 2, 1, CMP_STRIDE * HEAD_DIM)

        proj = _in_proj(x2d, norm_w[l][None, :], w_pack)
        q_r, kcvc, k_slc, k_win, v_slc_t, v_win_t = _prep(proj, cos_t, sin_t, b, s)
        kcvc = kcvc.reshape(b, 2, N_KV_GROUPS, n_chunks, CMP_STRIDE * HEAD_DIM)
        k_cmp, v_cmp_t = _compress(kcvc, w1, w2, pe, cos_c, sin_c)
        o_attn = _attention(q_r, k_cmp, v_cmp_t, k_slc, v_slc_t, k_win, v_win_t, proj, b, s)
        y_ssm = _ssm(proj, conv_w[l], conv_b[l][None, :], _dt_lanes(dt_bias[l]),
                     _dt_lanes(a_log[l]), _dt_lanes(d_skip[l]),
                     norm_ssm_w[l][None, :], tri, b, s)
        x2d = _out_proj(o_attn.reshape(b * s, d), proj, y_ssm, x2d, norm_attn_w[l][None, :],
                        w_out[l].astype(BF16), final_norm_w[None, :], final=(l == depth - 1))
    return x2d.reshape(b, s, d)
```

```python
import functools

import jax
import jax.numpy as jnp
from jax import lax
from jax.experimental import pallas as pl
from jax.experimental.pallas import tpu as pltpu

F32 = jnp.float32
BF16 = jnp.bfloat16

D_MODEL = 2048
HEAD_DIM = 128
N_HEADS_ATTN = 16
N_KV_GROUPS = 2
HEADS_PER_GROUP = 8
GROUP_WIDTH = HEADS_PER_GROUP * HEAD_DIM
CMP_STRIDE = 16
CMP_BLOCK = 32
CMP_HIDDEN = 256
SLC_BLOCK = 64
N_SELECT = 16
WINDOW = 512
Q_BLOCK = 128
ROPE_THETA = 10000.0
NEG_INF = -1e30
SEL_FORCED = 1e4
N_HEADS_SSM = 32
SSM_HEAD_DIM = 64
N_SSM_GROUPS = 8
SSM_HEADS_PER_GROUP = 4
D_STATE = 128
D_SSM = 2048
CONV_WIDTH = 4
CONV_DIM = 4096
SSD_CHUNK = 256
EPS = 1e-6
Q_SCALE = HEAD_DIM ** -0.5 * 1.4426950408889634

KV_COLS = 6 * N_KV_GROUPS * HEAD_DIM
GD_BLOCK = KV_COLS // 128
GATE_LANES = 3 * N_HEADS_ATTN
DT_LANE0 = GATE_LANES
D_PACK = 12288
PROJ_TM = 512
PROJ_TN = 2048
PREP_TQ = 256
OUT_TM = 256
CMP_CHUNK = 256
SEL_CHUNK = 256
BLOCK_LANES = 128
BLOCK_SPAN = BLOCK_LANES * SLC_BLOCK
WIN_SPAN = WINDOW + Q_BLOCK
VMEM_LIMIT = 56 * 1024 * 1024


def _dot(a, b):
    return jnp.dot(a, b, preferred_element_type=F32)


def _dot_nt(a, b):
    return lax.dot_general(a, b, (((1,), (1,)), ((), ())), preferred_element_type=F32)


def _split3(x):
    hi = x.astype(BF16)
    r1 = x - hi.astype(F32)
    mid = r1.astype(BF16)
    lo = (r1 - mid.astype(F32)).astype(BF16)
    return hi, mid, lo


def _sigmoid(x):
    return 1.0 / (1.0 + jnp.exp(-x))


def _params(sem):
    return pltpu.CompilerParams(dimension_semantics=sem, vmem_limit_bytes=VMEM_LIMIT)


def _rope_table_kernel(pos_ref, inv_ref, cos_ref, sin_ref):
    ang = pos_ref[0] * inv_ref[...]
    lane = lax.broadcasted_iota(jnp.int32, ang.shape, 1)
    cos_ref[0] = jnp.cos(ang)
    s = jnp.sin(ang)
    sin_ref[0] = jnp.where(lane < HEAD_DIM // 2, -s, s)


def _rope_tables(pos_f, inv_full):
    b, n, _ = pos_f.shape
    rows = n if n <= 2048 else 2048
    out = jax.ShapeDtypeStruct((b, n, HEAD_DIM), F32)
    return pl.pallas_call(
        _rope_table_kernel,
        grid=(b, n // rows),
        in_specs=[pl.BlockSpec((1, rows, 1), lambda i, j: (i, j, 0)),
                  pl.BlockSpec((1, HEAD_DIM), lambda i, j: (0, 0))],
        out_specs=[pl.BlockSpec((1, rows, HEAD_DIM), lambda i, j: (i, j, 0))] * 2,
        out_shape=[out, out],
        compiler_params=_params(("parallel", "parallel")),
        name="rope_tables",
    )(pos_f, inv_full)


def _proj_kernel(x_ref, nw_ref, w_ref, o_ref, h_ref):
    @pl.when(pl.program_id(1) == 0)
    def _():
        x = x_ref[...]
        ms = jnp.mean(x * x, axis=-1, keepdims=True)
        h_ref[...] = (x * lax.rsqrt(ms + EPS) * nw_ref[...]).astype(BF16)

    o_ref[...] = _dot(h_ref[...], w_ref[...])


def _in_proj(x2d, norm_w, w_pack):
    t = x2d.shape[0]
    return pl.pallas_call(
        _proj_kernel,
        grid=(t // PROJ_TM, D_PACK // PROJ_TN),
        in_specs=[pl.BlockSpec((PROJ_TM, D_MODEL), lambda i, j: (i, 0)),
                  pl.BlockSpec((1, D_MODEL), lambda i, j: (0, 0)),
                  pl.BlockSpec((D_MODEL, PROJ_TN), lambda i, j: (0, j))],
        out_specs=pl.BlockSpec((PROJ_TM, PROJ_TN), lambda i, j: (i, j)),
        out_shape=jax.ShapeDtypeStruct((t, D_PACK), F32),
        scratch_shapes=[pltpu.VMEM((PROJ_TM, D_MODEL), BF16)],
        compiler_params=_params(("parallel", "arbitrary")),
        name="in_proj",
    )(x2d, norm_w, w_pack)


def _prep_kernel(kv_ref, q_ref, cos_ref, sin_ref,
                 qo_ref, kcvc_ref, ks_ref, kw_ref, vst_ref, vwt_ref):
    c = cos_ref[0]
    s = sin_ref[0]

    def rope(u):
        return u * c + pltpu.roll(u, HEAD_DIM // 2, 1) * s

    for h in range(N_HEADS_ATTN):
        sl = slice(HEAD_DIM * h, HEAD_DIM * (h + 1))
        qo_ref[0, h] = (rope(q_ref[:, sl]) * Q_SCALE).T.astype(BF16)

    def seg(i, g):
        lo = (i * N_KV_GROUPS + g) * HEAD_DIM
        return kv_ref[:, lo:lo + HEAD_DIM]

    tq = q_ref.shape[0]
    tok = pl.program_id(1) * tq + lax.broadcasted_iota(jnp.int32, (tq, BLOCK_LANES), 0)
    lane = lax.broadcasted_iota(jnp.int32, (tq, BLOCK_LANES), 1)
    block_onehot = jnp.where((tok // SLC_BLOCK) % BLOCK_LANES == lane, 1.0, 0.0).astype(BF16)

    for g in range(N_KV_GROUPS):
        kcvc_ref[0, 0, g] = seg(0, g).astype(BF16)
        kcvc_ref[0, 1, g] = seg(1, g).astype(BF16)
        ks_ref[0, g, :, 0:HEAD_DIM] = rope(seg(2, g)).astype(BF16)
        ks_ref[0, g, :, HEAD_DIM:HEAD_DIM + BLOCK_LANES] = block_onehot
        vst_ref[0, g] = seg(3, g).T.astype(BF16)
        kw_ref[0, g] = rope(seg(4, g)).astype(BF16)
        vwt_ref[0, g] = seg(5, g).T.astype(BF16)


def _prep(proj, cos_t, sin_t, b, s):
    nq = s // PREP_TQ
    g = N_KV_GROUPS
    row = lambda i, j: (i * nq + j, 0)
    kshape = jax.ShapeDtypeStruct((b, g, s, HEAD_DIM), BF16)
    vshape = jax.ShapeDtypeStruct((b, g, HEAD_DIM, s), BF16)
    kspec = pl.BlockSpec((1, g, PREP_TQ, HEAD_DIM), lambda i, j: (i, 0, j, 0))
    vspec = pl.BlockSpec((1, g, HEAD_DIM, PREP_TQ), lambda i, j: (i, 0, 0, j))
    ks_width = HEAD_DIM + BLOCK_LANES
    ks_shape = jax.ShapeDtypeStruct((b, g, s, ks_width), BF16)
    ks_spec = pl.BlockSpec((1, g, PREP_TQ, ks_width), lambda i, j: (i, 0, j, 0))
    return pl.pallas_call(
        _prep_kernel,
        grid=(b, nq),
        in_specs=[pl.BlockSpec((PREP_TQ, KV_COLS), row),
                  pl.BlockSpec((PREP_TQ, D_MODEL), lambda i, j: (i * nq + j, 1)),
                  pl.BlockSpec((1, PREP_TQ, HEAD_DIM), lambda i, j: (i, j, 0)),
                  pl.BlockSpec((1, PREP_TQ, HEAD_DIM), lambda i, j: (i, j, 0))],
        out_specs=[pl.BlockSpec((1, N_HEADS_ATTN, HEAD_DIM, PREP_TQ), lambda i, j: (i, 0, 0, j)),
                   pl.BlockSpec((1, 2, g, PREP_TQ, HEAD_DIM), lambda i, j: (i, 0, 0, j, 0)),
                   ks_spec, kspec, vspec, vspec],
        out_shape=[jax.ShapeDtypeStruct((b, N_HEADS_ATTN, HEAD_DIM, s), BF16),
                   jax.ShapeDtypeStruct((b, 2, g, s, HEAD_DIM), BF16),
                   ks_shape, kshape, vshape, vshape],
        compiler_params=_params(("parallel", "parallel")),
        name="rope_relayout",
    )(proj, proj, cos_t, sin_t)


def _cmp_kernel(x_ref, w1_ref, w2_ref, pe_ref, cos_ref, sin_ref, kc_ref, vct_ref):
    n = x_ref.shape[3]
    for kv in range(2):
        x = x_ref[0, kv, 0]
        a0 = _dot(x, w1_ref[kv, 0])
        a1 = _dot(x, w1_ref[kv, 1])
        bias = jnp.zeros((8, CMP_HIDDEN), F32)
        for r in range(2):
            pe = jnp.broadcast_to(pe_ref[kv, r], (8, CMP_STRIDE * HEAD_DIM))
            for part in _split3(pe):
                bias = bias + _dot(part, w1_ref[kv, r])
        hid = a0 + pltpu.roll(a1, n - 1, 0) + bias[0:1]
        act = hid * _sigmoid(hid)
        out = _dot(act.astype(BF16), w2_ref[kv])
        if kv == 0:
            roped = out * cos_ref[0] + pltpu.roll(out, HEAD_DIM // 2, 1) * sin_ref[0]
            kc_ref[0, 0] = roped.astype(BF16)
        else:
            vct_ref[0, 0] = out.T.astype(BF16)


def _compress(kcvc, w1, w2, pe, cos_c, sin_c):
    b, _, g, n, width = kcvc.shape
    return pl.pallas_call(
        _cmp_kernel,
        grid=(b, g),
        in_specs=[pl.BlockSpec((1, 2, 1, n, width), lambda i, j: (i, 0, j, 0, 0)),
                  pl.BlockSpec(w1.shape, lambda i, j: (0, 0, 0, 0)),
                  pl.BlockSpec(w2.shape, lambda i, j: (0, 0, 0)),
                  pl.BlockSpec(pe.shape, lambda i, j: (0, 0, 0, 0)),
                  pl.BlockSpec((1, n, HEAD_DIM), lambda i, j: (i, 0, 0)),
                  pl.BlockSpec((1, n, HEAD_DIM), lambda i, j: (i, 0, 0))],
        out_specs=[pl.BlockSpec((1, 1, n, HEAD_DIM), lambda i, j: (i, j, 0, 0)),
                   pl.BlockSpec((1, 1, HEAD_DIM, n), lambda i, j: (i, j, 0, 0))],
        out_shape=[jax.ShapeDtypeStruct((b, g, n, HEAD_DIM), BF16),
                   jax.ShapeDtypeStruct((b, g, HEAD_DIM, n), BF16)],
        compiler_params=_params(("parallel", "parallel")),
        name="compress",
    )(kcvc, w1, w2, pe, cos_c, sin_c)


def _attn_kernel(q_ref, kc_ref, vct_ref, ks_ref, vst_ref, kw_ref, vwt_ref, gd_ref, o_ref,
                 sc_ref, imp_ref, sel_ref, acc_ref, gt_ref, sa_ref, sb_ref, qp_ref):
    n_cmp = kc_ref.shape[2]
    n_blk = imp_ref.shape[0] // (SLC_BLOCK // CMP_STRIDE)
    g = pl.program_id(1)
    qi = pl.program_id(2)
    t0 = qi * Q_BLOCK
    lanes_all = GROUP_WIDTH

    qt = jnp.concatenate([q_ref[0, r] for r in range(HEADS_PER_GROUP)], axis=1)
    t_lane = t0 + lax.broadcasted_iota(jnp.int32, (1, Q_BLOCK), 1)

    def tile_heads(x):
        return jnp.concatenate([x] * HEADS_PER_GROUP, axis=1)

    neg_row = jnp.full((1, lanes_all), NEG_INF, F32)
    zero_row = jnp.zeros((1, lanes_all), F32)

    chunk = min(CMP_CHUNK, n_cmp)
    n_c = (8 * qi + 6) // chunk + 1
    imp_ref[...] = jnp.zeros_like(imp_ref)
    acc_ref[0] = jnp.zeros((HEAD_DIM, lanes_all), F32)

    def cmp_scores(c, carry):
        m, l = carry
        off = pl.multiple_of(c * chunk, chunk)
        ci = off + lax.broadcasted_iota(jnp.int32, (chunk, Q_BLOCK), 0)
        vis = ci * CMP_STRIDE + (CMP_BLOCK - 1) <= t_lane
        s = _dot(kc_ref[0, 0, pl.ds(off, chunk), :], qt) + tile_heads(jnp.where(vis, 0.0, NEG_INF))
        sc_ref[pl.ds(off, chunk), :] = s
        m_new = jnp.maximum(m, jnp.max(s, axis=0, keepdims=True))
        l = l * jnp.exp2(m - m_new) + jnp.sum(jnp.exp2(s - m_new), axis=0, keepdims=True)
        return m_new, l

    m_c, l_c = lax.fori_loop(0, n_c, cmp_scores, (neg_row, zero_row))
    w_c = jnp.where(m_c > 0.5 * NEG_INF, 1.0 / l_c, 0.0)

    def cmp_out(c, carry):
        off = pl.multiple_of(c * chunk, chunk)
        pn = jnp.exp2(sc_ref[pl.ds(off, chunk), :] - m_c) * w_c
        imp = pn[:, 0:Q_BLOCK]
        for r in range(1, HEADS_PER_GROUP):
            imp = imp + pn[:, Q_BLOCK * r:Q_BLOCK * (r + 1)]
        imp_ref[pl.ds(off, chunk), :] = imp
        acc_ref[0] += _dot(vct_ref[0, 0, :, pl.ds(off, chunk)], pn.astype(BF16))
        return carry

    lax.fori_loop(0, n_c, cmp_out, 0)

    ratio = SLC_BLOCK // CMP_STRIDE
    score_in = pltpu.roll(imp_ref[pl.ds(ratio - 1, n_blk, stride=ratio), :], 1, 0)
    for m in range(ratio):
        score_in = score_in + imp_ref[pl.ds(m, n_blk, stride=ratio), :]
    blk = lax.broadcasted_iota(jnp.int32, (n_blk, Q_BLOCK), 0)
    cur = t_lane // SLC_BLOCK
    forced = (blk == 0) | (blk == cur) | (blk == cur - 1)
    visible = blk <= cur
    score0 = jnp.where(forced, -jnp.inf, jnp.where(visible, score_in, -SEL_FORCED))
    blk_f = blk.astype(F32)

    def pick(_, carry):
        score, sel = carry
        mx = jnp.max(score, axis=0, keepdims=True)
        first = jnp.min(jnp.where(score == mx, blk_f, float(n_blk)), axis=0, keepdims=True)
        hit = blk_f == first
        return jnp.where(hit, -jnp.inf, score), jnp.where(hit, 1.0, sel)

    _, sel = lax.fori_loop(0, min(N_SELECT, n_blk) - 3, pick, (score0, jnp.where(forced, 1.0, 0.0)))
    sel_ref[...] = jnp.zeros_like(sel_ref)
    sel_ref[0:n_blk, :] = jnp.where(visible, sel, 0.0)

    for h in range(qp_ref.shape[0]):
        picked = sel_ref[BLOCK_LANES * h:BLOCK_LANES * (h + 1), :]
        qp_ref[h, 0:HEAD_DIM, :] = qt
        qp_ref[h, HEAD_DIM:HEAD_DIM + BLOCK_LANES, :] = tile_heads(
            jnp.where(picked > 0.5, 0.0, NEG_INF).astype(BF16))

    def finish_scores(s, s_ref, n):
        s_ref[0:n, :] = s
        return jnp.max(s, axis=0, keepdims=True)

    def causal_bias(off, n):
        kpos = off + lax.broadcasted_iota(jnp.int32, (n, Q_BLOCK), 0)
        return tile_heads(jnp.where(kpos <= t_lane, 0.0, NEG_INF))

    def sel_scores(off, s_ref, causal):
        s = _dot(ks_ref[0, 0, pl.ds(off, SEL_CHUNK), :], qp_ref[off // BLOCK_SPAN])
        if causal:
            s = s + causal_bias(off, SEL_CHUNK)
        return finish_scores(s, s_ref, SEL_CHUNK)

    def win_scores(off, s_ref, n):
        kpos = off + lax.broadcasted_iota(jnp.int32, (n, Q_BLOCK), 0)
        in_win = (kpos <= t_lane) & (kpos > t_lane - WINDOW)
        s = _dot(kw_ref[0, 0, pl.ds(off, n), :], qt) + tile_heads(jnp.where(in_win, 0.0, NEG_INF))
        return finish_scores(s, s_ref, n)

    def value_tile(vt, s_ref, n, tile_max, m, l, branch):
        m_new = jnp.maximum(m, tile_max)
        alpha = jnp.exp2(m - m_new)
        p = jnp.exp2(s_ref[0:n, :] - m_new)
        acc_ref[branch] = alpha * acc_ref[branch] + _dot(vt, p.astype(BF16))
        return m_new, alpha * l + jnp.sum(p, axis=0, keepdims=True)

    def sel_values(off, s_ref, tile_max, m, l):
        return value_tile(vst_ref[0, 0, :, pl.ds(off, SEL_CHUNK)], s_ref, SEL_CHUNK, tile_max, m, l, 1)

    trip = 2 * SEL_CHUNK
    n_k = (t0 + Q_BLOCK - 1) // trip + 1
    acc_ref[1] = jnp.zeros((HEAD_DIM, lanes_all), F32)

    def sel_step(c, carry):
        m, l, max_a = carry
        off_a = pl.multiple_of(c * trip, trip)
        off_b = pl.multiple_of(off_a + SEL_CHUNK, SEL_CHUNK)
        max_b = sel_scores(off_b, sb_ref, False)
        m, l = sel_values(off_a, sa_ref, max_a, m, l)
        max_a = sel_scores(pl.multiple_of(off_a + trip, trip), sa_ref, False)
        m, l = sel_values(off_b, sb_ref, max_b, m, l)
        return m, l, max_a

    max_a0 = sel_scores(0, sa_ref, False)
    n_main = n_k - 1
    carry = lax.fori_loop(0, n_main // 2, lambda i, cr: sel_step(2 * i + 1, sel_step(2 * i, cr)),
                          (neg_row, zero_row, max_a0))
    m_s, l_s, _ = lax.fori_loop(n_main - n_main % 2, n_main, sel_step, carry)

    off_a = pl.multiple_of((n_k - 1) * trip, trip)
    off_b = pl.multiple_of(off_a + SEL_CHUNK, SEL_CHUNK)
    ws = pl.multiple_of(jnp.maximum(t0 - WINDOW, 0), Q_BLOCK)
    max_a = sel_scores(off_a, sa_ref, True)
    max_b = sel_scores(off_b, sb_ref, True)
    m_s, l_s = sel_values(off_a, sa_ref, max_a, m_s, l_s)
    max_w0 = win_scores(ws, sa_ref, SEL_CHUNK)
    m_s, l_s = sel_values(off_b, sb_ref, max_b, m_s, l_s)
    inv_s = 1.0 / l_s

    acc_ref[2] = jnp.zeros((HEAD_DIM, lanes_all), F32)
    last_n = WIN_SPAN - 2 * SEL_CHUNK
    ws1 = pl.multiple_of(ws + SEL_CHUNK, Q_BLOCK)
    ws2 = pl.multiple_of(ws + 2 * SEL_CHUNK, Q_BLOCK)
    max_w1 = win_scores(ws1, sb_ref, SEL_CHUNK)
    m_w, l_w = value_tile(vwt_ref[0, 0, :, pl.ds(ws, SEL_CHUNK)], sa_ref, SEL_CHUNK, max_w0, neg_row, zero_row, 2)
    max_w2 = win_scores(ws2, sa_ref, last_n)
    m_w, l_w = value_tile(vwt_ref[0, 0, :, pl.ds(ws1, SEL_CHUNK)], sb_ref, SEL_CHUNK, max_w1, m_w, l_w, 2)
    m_w, l_w = value_tile(vwt_ref[0, 0, :, pl.ds(ws2, last_n)], sa_ref, last_n, max_w2, m_w, l_w, 2)
    inv_w = 1.0 / l_w

    gt_ref[...] = _sigmoid(gd_ref[...]).T
    for r in range(HEADS_PER_GROUP):
        sl = slice(Q_BLOCK * r, Q_BLOCK * (r + 1))
        base = 3 * (HEADS_PER_GROUP * g + r)
        o_t = (gt_ref[pl.ds(base, 1), :] * acc_ref[0, :, sl]
               + gt_ref[pl.ds(base + 1, 1), :] * (acc_ref[1, :, sl] * inv_s[:, sl])
               + gt_ref[pl.ds(base + 2, 1), :] * (acc_ref[2, :, sl] * inv_w[:, sl]))
        o_ref[0, :, HEAD_DIM * r:HEAD_DIM * (r + 1)] = o_t.T


def _attention(q_r, k_cmp, v_cmp_t, k_slc, v_slc_t, k_win, v_win_t, proj, b, s):
    g = N_KV_GROUPS
    nqb = s // Q_BLOCK
    n_cmp = s // CMP_STRIDE
    n_blk = s // SLC_BLOCK
    n_bias = pl.cdiv(n_blk, BLOCK_LANES)
    once = pl.Buffered(1)
    kspec = lambda n, w: pl.BlockSpec((1, 1, n, w), lambda i, j, k: (i, j, 0, 0), pipeline_mode=once)
    vspec = lambda n: pl.BlockSpec((1, 1, HEAD_DIM, n), lambda i, j, k: (i, j, 0, 0), pipeline_mode=once)
    return pl.pallas_call(
        _attn_kernel,
        grid=(b, g, nqb),
        in_specs=[pl.BlockSpec((1, HEADS_PER_GROUP, HEAD_DIM, Q_BLOCK), lambda i, j, k: (i, j, 0, k)),
                  kspec(n_cmp, HEAD_DIM), vspec(n_cmp), kspec(s, HEAD_DIM + BLOCK_LANES), vspec(s),
                  kspec(s, HEAD_DIM), vspec(s),
                  pl.BlockSpec((Q_BLOCK, 128), lambda i, j, k: (i * nqb + k, GD_BLOCK))],
        out_specs=pl.BlockSpec((1, Q_BLOCK, GROUP_WIDTH), lambda i, j, k: (i, k, j)),
        out_shape=jax.ShapeDtypeStruct((b, s, D_MODEL), F32),
        scratch_shapes=[pltpu.VMEM((n_cmp, GROUP_WIDTH), F32),
                        pltpu.VMEM((n_cmp, Q_BLOCK), F32),
                        pltpu.VMEM((n_bias * BLOCK_LANES, Q_BLOCK), F32),
                        pltpu.VMEM((3, HEAD_DIM, GROUP_WIDTH), F32),
                        pltpu.VMEM((128, Q_BLOCK), F32),
                        pltpu.VMEM((SEL_CHUNK, GROUP_WIDTH), F32),
                        pltpu.VMEM((SEL_CHUNK, GROUP_WIDTH), F32),
                        pltpu.VMEM((n_bias, HEAD_DIM + BLOCK_LANES, GROUP_WIDTH), BF16)],
        compiler_params=_params(("parallel", "parallel", "arbitrary")),
        name="nsa_attention",
    )(q_r, k_cmp, v_cmp_t, k_slc, v_slc_t, k_win, v_win_t, proj)


def _ssm_kernel(xbc_ref, gd_ref, z_ref, cw_ref, cb_ref, dtb_ref, alog_ref, dsk_ref, nw_ref, tri_ref,
                y_ref, tail_ref, st_ref, u_ref):
    L = SSD_CHUNK

    @pl.when(pl.program_id(1) == 0)
    def _():
        tail_ref[...] = jnp.zeros_like(tail_ref)
        st_ref[...] = jnp.zeros_like(st_ref)

    strip = 512
    for cs in range(CONV_DIM // strip):
        sl = slice(strip * cs, strip * (cs + 1))
        cur = xbc_ref[:, sl]
        ext = jnp.concatenate([tail_ref[:, sl], cur], axis=0)
        acc = cb_ref[:, sl] + cw_ref[0:1, sl] * ext[5:5 + L]
        for k in range(1, CONV_WIDTH):
            acc = acc + cw_ref[k:k + 1, sl] * ext[5 + k:5 + k + L]
        tail_ref[:, sl] = cur[L - 8:L]
        u_ref[:, sl] = acc * _sigmoid(acc)

    x_dt = gd_ref[...] + dtb_ref[...]
    dt = jnp.maximum(x_dt, 0.0) + jnp.log1p(jnp.exp(-jnp.abs(x_dt)))
    da = dt * -jnp.exp(alog_ref[...])
    tri = tri_ref[...]
    cum = jnp.zeros((L, 128), F32)
    for part in _split3(da):
        cum = cum + _dot(tri, part)
    cum_t = cum.T
    row = lax.broadcasted_iota(jnp.int32, (L, L), 0)
    col = lax.broadcasted_iota(jnp.int32, (L, L), 1)
    causal = row >= col

    gw = SSM_HEADS_PER_GROUP * SSM_HEAD_DIM
    for g in range(N_SSM_GROUPS):
        b_g = u_ref[:, D_SSM + D_STATE * g:D_SSM + D_STATE * (g + 1)]
        c_g = u_ref[:, D_SSM + N_SSM_GROUPS * D_STATE + D_STATE * g:
                    D_SSM + N_SSM_GROUPS * D_STATE + D_STATE * (g + 1)]
        b_b = b_g.astype(BF16)
        c_b = c_g.astype(BF16)
        cb = _dot_nt(c_b, b_b)
        b_t = b_g.T.astype(BF16)
        st = st_ref[g]
        y_off = _dot(c_b, st.astype(BF16))
        ys, xends, decs = [], [], []
        for r in range(SSM_HEADS_PER_GROUP):
            h = SSM_HEADS_PER_GROUP * g + r
            lane = DT_LANE0 + h
            cc = cum[:, lane:lane + 1]
            cr = cum_t[lane:lane + 1, :]
            decay = jnp.exp(jnp.where(causal, cc - cr, -jnp.inf))
            w = (cb * decay).astype(BF16)
            xs_h = u_ref[:, SSM_HEAD_DIM * h:SSM_HEAD_DIM * (h + 1)]
            xdt = xs_h * dt[:, lane:lane + 1]
            y = (_dot(w, xdt.astype(BF16))
                 + y_off[:, SSM_HEAD_DIM * r:SSM_HEAD_DIM * (r + 1)] * jnp.exp(cc)
                 + dsk_ref[:, lane:lane + 1] * xs_h)
            last = cr[:, L - 1:L]
            xends.append(jnp.exp(last - cc) * xdt)
            decs.append(jnp.broadcast_to(jnp.exp(last), (1, SSM_HEAD_DIM)))
            ys.append(y)
        x_end = jnp.concatenate(xends, axis=1)
        st_ref[g] = st * jnp.concatenate(decs, axis=1) + _dot(b_t, x_end.astype(BF16))
        y_g = jnp.concatenate(ys, axis=1)
        sl = slice(gw * g, gw * (g + 1))
        z_g = z_ref[:, sl]
        v = y_g * (z_g * _sigmoid(z_g))
        ms = jnp.mean(v * v, axis=-1, keepdims=True)
        y_ref[:, sl] = (v * lax.rsqrt(ms + EPS) * nw_ref[:, sl]).astype(BF16)


def _ssm(proj, conv_w, conv_b, dtb, alog, dsk, norm_w, tri, b, s):
    nc = s // SSD_CHUNK
    full = lambda shape: pl.BlockSpec(shape, lambda i, j: (0,) * len(shape))
    return pl.pallas_call(
        _ssm_kernel,
        grid=(b, nc),
        in_specs=[pl.BlockSpec((SSD_CHUNK, CONV_DIM), lambda i, j: (i * nc + j, 1)),
                  pl.BlockSpec((SSD_CHUNK, 128), lambda i, j: (i * nc + j, GD_BLOCK)),
                  pl.BlockSpec((SSD_CHUNK, D_SSM), lambda i, j: (i * nc + j, 5)),
                  full((CONV_WIDTH, CONV_DIM)), full((1, CONV_DIM)),
                  full((1, 128)), full((1, 128)), full((1, 128)),
                  full((1, D_SSM)), full((SSD_CHUNK, SSD_CHUNK))],
        out_specs=pl.BlockSpec((SSD_CHUNK, D_SSM), lambda i, j: (i * nc + j, 0)),
        out_shape=jax.ShapeDtypeStruct((b * s, D_SSM), BF16),
        scratch_shapes=[pltpu.VMEM((8, CONV_DIM), F32),
                        pltpu.VMEM((N_SSM_GROUPS, D_STATE, SSM_HEADS_PER_GROUP * SSM_HEAD_DIM), F32),
                        pltpu.VMEM((SSD_CHUNK, CONV_DIM), F32)],
        compiler_params=_params(("parallel", "arbitrary")),
        name="ssd_scan",
    )(proj, proj, proj, conv_w, conv_b, dtb, alog, dsk, norm_w, tri)


def _out_kernel(o_ref, z_ref, ys_ref, x_ref, naw_ref, w_ref, fw_ref, out_ref, *, final):
    z = z_ref[...]
    v = o_ref[...] * (z * _sigmoid(z))
    ms = jnp.mean(v * v, axis=-1, keepdims=True)
    ya = (v * lax.rsqrt(ms + EPS) * naw_ref[...]).astype(BF16)
    acc = x_ref[...] + _dot(ya, w_ref[0:D_MODEL, :]) + _dot(ys_ref[...], w_ref[D_MODEL:2 * D_MODEL, :])
    if final:
        ms2 = jnp.mean(acc * acc, axis=-1, keepdims=True)
        acc = acc * lax.rsqrt(ms2 + EPS) * fw_ref[...]
    out_ref[...] = acc


def _out_proj(o_attn, proj, y_ssm, x2d, norm_attn_w, w_out, final_w, final):
    t = x2d.shape[0]
    rowblk = lambda c: pl.BlockSpec((OUT_TM, D_MODEL), lambda i: (i, c))
    vec = pl.BlockSpec((1, D_MODEL), lambda i: (0, 0))
    return pl.pallas_call(
        functools.partial(_out_kernel, final=final),
        grid=(t // OUT_TM,),
        in_specs=[rowblk(0), rowblk(4), rowblk(0), rowblk(0), vec,
                  pl.BlockSpec((2 * D_MODEL, D_MODEL), lambda i: (0, 0), pipeline_mode=pl.Buffered(1)),
                  vec],
        out_specs=rowblk(0),
        out_shape=jax.ShapeDtypeStruct((t, D_MODEL), F32),
        compiler_params=_params(("parallel",)),
        name="out_proj",
    )(o_attn, proj, y_ssm, x2d, norm_attn_w, w_out, final_w)


def _pack_w_in(w):
    q0, kv0, gate0, za0, xbc0, dt0, zs0, end = 0, 2048, 3584, 3632, 5680, 9776, 9808, 11856
    pad = jnp.zeros((w.shape[0], 2048 - KV_COLS - GATE_LANES - N_HEADS_SSM), w.dtype)
    return jnp.concatenate(
        [w[:, kv0:gate0], w[:, gate0:za0], w[:, dt0:zs0], pad,
         w[:, q0:kv0], w[:, xbc0:dt0], w[:, za0:xbc0], w[:, zs0:end]], axis=1).astype(BF16)


def _dt_lanes(v):
    return jnp.zeros((1, 128), F32).at[0, DT_LANE0:DT_LANE0 + N_HEADS_SSM].set(v.astype(F32))


def kernel(x, positions, norm_w, w_in, cmp_pe_k, cmp_w1_k, cmp_w2_k, cmp_pe_v, cmp_w1_v, cmp_w2_v,
           conv_w, conv_b, dt_bias, a_log, d_skip, norm_attn_w, norm_ssm_w, w_out, final_norm_w):
    b, s, d = x.shape
    depth = w_in.shape[0]
    assert d == D_MODEL and s % 2048 == 0
    n_chunks = s // CMP_STRIDE

    half = HEAD_DIM // 2
    inv_freq = ROPE_THETA ** (-jnp.arange(half, dtype=F32) / half)
    inv_full = jnp.concatenate([inv_freq, inv_freq])[None, :]
    pos_f = positions.astype(F32)
    cos_t, sin_t = _rope_tables(pos_f[:, :, None], inv_full)
    pos_c = jnp.concatenate([pos_f[:, CMP_BLOCK - 1::CMP_STRIDE], pos_f[:, -1:]], axis=1)
    cos_c, sin_c = _rope_tables(pos_c[:, :, None], inv_full)

    tri = jnp.tril(jnp.ones((SSD_CHUNK, SSD_CHUNK), F32)).astype(BF16)
    x2d = x.reshape(b * s, d)
    for l in range(depth):
        w_pack = _pack_w_in(w_in[l])
        w1 = jnp.stack([cmp_w1_k[l], cmp_w1_v[l]]).reshape(2, 2, CMP_STRIDE * HEAD_DIM, CMP_HIDDEN).astype(BF16)
        w2 = jnp.stack([cmp_w2_k[l], cmp_w2_v[l]]).astype(BF16)
        pe = jnp.stack([cmp_pe_k[l], cmp_pe_v[l]]).reshape(2, 2, 1, CMP_STRIDE * HEAD_DIM)

        proj = _in_proj(x2d, norm_w[l][None, :], w_pack)
        q_r, kcvc, k_slc, k_win, v_slc_t, v_win_t = _prep(proj, cos_t, sin_t, b, s)
        kcvc = kcvc.reshape(b, 2, N_KV_GROUPS, n_chunks, CMP_STRIDE * HEAD_DIM)
        k_cmp, v_cmp_t = _compress(kcvc, w1, w2, pe, cos_c, sin_c)
        o_attn = _attention(q_r, k_cmp, v_cmp_t, k_slc, v_slc_t, k_win, v_win_t, proj, b, s)
        y_ssm = _ssm(proj, conv_w[l], conv_b[l][None, :], _dt_lanes(dt_bias[l]),
                     _dt_lanes(a_log[l]), _dt_lanes(d_skip[l]),
                     norm_ssm_w[l][None, :], tri, b, s)
        x2d = _out_proj(o_attn.reshape(b * s, d), proj, y_ssm, x2d, norm_attn_w[l][None, :],
                        w_out[l].astype(BF16), final_norm_w[None, :], final=(l == depth - 1))
    return x2d.reshape(b, s, d)
```

```python
import functools

import jax
import jax.numpy as jnp
from jax import lax
from jax.experimental import pallas as pl
from jax.experimental.pallas import tpu as pltpu

F32 = jnp.float32
BF16 = jnp.bfloat16

D_MODEL = 2048
HEAD_DIM = 128
N_HEADS_ATTN = 16
N_KV_GROUPS = 2
HEADS_PER_GROUP = 8
GROUP_WIDTH = HEADS_PER_GROUP * HEAD_DIM
CMP_STRIDE = 16
CMP_BLOCK = 32
CMP_HIDDEN = 256
SLC_BLOCK = 64
N_SELECT = 16
WINDOW = 512
Q_BLOCK = 128
ROPE_THETA = 10000.0
NEG_INF = -1e30
SEL_FORCED = 1e4
N_HEADS_SSM = 32
SSM_HEAD_DIM = 64
N_SSM_GROUPS = 8
SSM_HEADS_PER_GROUP = 4
D_STATE = 128
D_SSM = 2048
CONV_WIDTH = 4
CONV_DIM = 4096
SSD_CHUNK = 256
EPS = 1e-6
Q_SCALE = HEAD_DIM ** -0.5 * 1.4426950408889634

KV_COLS = 6 * N_KV_GROUPS * HEAD_DIM
GD_BLOCK = KV_COLS // 128
GATE_LANES = 3 * N_HEADS_ATTN
DT_LANE0 = GATE_LANES
D_PACK = 12288
PROJ_TM = 512
PROJ_TN = 2048
PREP_TQ = 256
OUT_TM = 256
CMP_CHUNK = 256
SEL_CHUNK = 256
BLOCK_LANES = 128
BLOCK_SPAN = BLOCK_LANES * SLC_BLOCK
WIN_SPAN = WINDOW + Q_BLOCK
VMEM_LIMIT = 56 * 1024 * 1024


def _dot(a, b):
    return jnp.dot(a, b, preferred_element_type=F32)


def _dot_nt(a, b):
    return lax.dot_general(a, b, (((1,), (1,)), ((), ())), preferred_element_type=F32)


def _split3(x):
    hi = x.astype(BF16)
    r1 = x - hi.astype(F32)
    mid = r1.astype(BF16)
    lo = (r1 - mid.astype(F32)).astype(BF16)
    return hi, mid, lo


def _sigmoid(x):
    return 1.0 / (1.0 + jnp.exp(-x))


def _params(sem):
    return pltpu.CompilerParams(dimension_semantics=sem, vmem_limit_bytes=VMEM_LIMIT)


def _rope_table_kernel(pos_ref, inv_ref, cos_ref, sin_ref):
    ang = pos_ref[0] * inv_ref[...]
    lane = lax.broadcasted_iota(jnp.int32, ang.shape, 1)
    cos_ref[0] = jnp.cos(ang)
    s = jnp.sin(ang)
    sin_ref[0] = jnp.where(lane < HEAD_DIM // 2, -s, s)


def _rope_tables(pos_f, inv_full):
    b, n, _ = pos_f.shape
    rows = n if n <= 2048 else 2048
    out = jax.ShapeDtypeStruct((b, n, HEAD_DIM), F32)
    return pl.pallas_call(
        _rope_table_kernel,
        grid=(b, n // rows),
        in_specs=[pl.BlockSpec((1, rows, 1), lambda i, j: (i, j, 0)),
                  pl.BlockSpec((1, HEAD_DIM), lambda i, j: (0, 0))],
        out_specs=[pl.BlockSpec((1, rows, HEAD_DIM), lambda i, j: (i, j, 0))] * 2,
        out_shape=[out, out],
        compiler_params=_params(("parallel", "parallel")),
        name="rope_tables",
    )(pos_f, inv_full)


def _proj_kernel(x_ref, nw_ref, w_ref, o_ref, h_ref):
    @pl.when(pl.program_id(1) == 0)
    def _():
        x = x_ref[...]
        ms = jnp.mean(x * x, axis=-1, keepdims=True)
        h_ref[...] = (x * lax.rsqrt(ms + EPS) * nw_ref[...]).astype(BF16)

    o_ref[...] = _dot(h_ref[...], w_ref[...])


def _in_proj(x2d, norm_w, w_pack):
    t = x2d.shape[0]
    return pl.pallas_call(
        _proj_kernel,
        grid=(t // PROJ_TM, D_PACK // PROJ_TN),
        in_specs=[pl.BlockSpec((PROJ_TM, D_MODEL), lambda i, j: (i, 0)),
                  pl.BlockSpec((1, D_MODEL), lambda i, j: (0, 0)),
                  pl.BlockSpec((D_MODEL, PROJ_TN), lambda i, j: (0, j))],
        out_specs=pl.BlockSpec((PROJ_TM, PROJ_TN), lambda i, j: (i, j)),
        out_shape=jax.ShapeDtypeStruct((t, D_PACK), F32),
        scratch_shapes=[pltpu.VMEM((PROJ_TM, D_MODEL), BF16)],
        compiler_params=_params(("parallel", "arbitrary")),
        name="in_proj",
    )(x2d, norm_w, w_pack)


def _prep_kernel(kv_ref, q_ref, cos_ref, sin_ref,
                 qo_ref, kcvc_ref, ks_ref, kw_ref, vst_ref, vwt_ref):
    c = cos_ref[0]
    s = sin_ref[0]

    def rope(u):
        return u * c + pltpu.roll(u, HEAD_DIM // 2, 1) * s

    for h in range(N_HEADS_ATTN):
        sl = slice(HEAD_DIM * h, HEAD_DIM * (h + 1))
        qo_ref[0, h] = (rope(q_ref[:, sl]) * Q_SCALE).T.astype(BF16)

    def seg(i, g):
        lo = (i * N_KV_GROUPS + g) * HEAD_DIM
        return kv_ref[:, lo:lo + HEAD_DIM]

    tq = q_ref.shape[0]
    tok = pl.program_id(1) * tq + lax.broadcasted_iota(jnp.int32, (tq, BLOCK_LANES), 0)
    lane = lax.broadcasted_iota(jnp.int32, (tq, BLOCK_LANES), 1)
    block_onehot = jnp.where((tok // SLC_BLOCK) % BLOCK_LANES == lane, 1.0, 0.0).astype(BF16)

    for g in range(N_KV_GROUPS):
        kcvc_ref[0, 0, g] = seg(0, g).astype(BF16)
        kcvc_ref[0, 1, g] = seg(1, g).astype(BF16)
        ks_ref[0, g, :, 0:HEAD_DIM] = rope(seg(2, g)).astype(BF16)
        ks_ref[0, g, :, HEAD_DIM:HEAD_DIM + BLOCK_LANES] = block_onehot
        vst_ref[0, g] = seg(3, g).T.astype(BF16)
        kw_ref[0, g] = rope(seg(4, g)).astype(BF16)
        vwt_ref[0, g] = seg(5, g).T.astype(BF16)


def _prep(proj, cos_t, sin_t, b, s):
    nq = s // PREP_TQ
    g = N_KV_GROUPS
    row = lambda i, j: (i * nq + j, 0)
    kshape = jax.ShapeDtypeStruct((b, g, s, HEAD_DIM), BF16)
    vshape = jax.ShapeDtypeStruct((b, g, HEAD_DIM, s), BF16)
    kspec = pl.BlockSpec((1, g, PREP_TQ, HEAD_DIM), lambda i, j: (i, 0, j, 0))
    vspec = pl.BlockSpec((1, g, HEAD_DIM, PREP_TQ), lambda i, j: (i, 0, 0, j))
    ks_width = HEAD_DIM + BLOCK_LANES
    ks_shape = jax.ShapeDtypeStruct((b, g, s, ks_width), BF16)
    ks_spec = pl.BlockSpec((1, g, PREP_TQ, ks_width), lambda i, j: (i, 0, j, 0))
    return pl.pallas_call(
        _prep_kernel,
        grid=(b, nq),
        in_specs=[pl.BlockSpec((PREP_TQ, KV_COLS), row),
                  pl.BlockSpec((PREP_TQ, D_MODEL), lambda i, j: (i * nq + j, 1)),
                  pl.BlockSpec((1, PREP_TQ, HEAD_DIM), lambda i, j: (i, j, 0)),
                  pl.BlockSpec((1, PREP_TQ, HEAD_DIM), lambda i, j: (i, j, 0))],
        out_specs=[pl.BlockSpec((1, N_HEADS_ATTN, HEAD_DIM, PREP_TQ), lambda i, j: (i, 0, 0, j)),
                   pl.BlockSpec((1, 2, g, PREP_TQ, HEAD_DIM), lambda i, j: (i, 0, 0, j, 0)),
                   ks_spec, kspec, vspec, vspec],
        out_shape=[jax.ShapeDtypeStruct((b, N_HEADS_ATTN, HEAD_DIM, s), BF16),
                   jax.ShapeDtypeStruct((b, 2, g, s, HEAD_DIM), BF16),
                   ks_shape, kshape, vshape, vshape],
        compiler_params=_params(("parallel", "parallel")),
        name="rope_relayout",
    )(proj, proj, cos_t, sin_t)


def _cmp_kernel(x_ref, w1_ref, w2_ref, pe_ref, cos_ref, sin_ref, kc_ref, vct_ref):
    n = x_ref.shape[3]
    for kv in range(2):
        x = x_ref[0, kv, 0]
        a0 = _dot(x, w1_ref[kv, 0])
        a1 = _dot(x, w1_ref[kv, 1])
        bias = jnp.zeros((8, CMP_HIDDEN), F32)
        for r in range(2):
            pe = jnp.broadcast_to(pe_ref[kv, r], (8, CMP_STRIDE * HEAD_DIM))
            for part in _split3(pe):
                bias = bias + _dot(part, w1_ref[kv, r])
        hid = a0 + pltpu.roll(a1, n - 1, 0) + bias[0:1]
        act = hid * _sigmoid(hid)
        out = _dot(act.astype(BF16), w2_ref[kv])
        if kv == 0:
            roped = out * cos_ref[0] + pltpu.roll(out, HEAD_DIM // 2, 1) * sin_ref[0]
            kc_ref[0, 0] = roped.astype(BF16)
        else:
            vct_ref[0, 0] = out.T.astype(BF16)


def _compress(kcvc, w1, w2, pe, cos_c, sin_c):
    b, _, g, n, width = kcvc.shape
    return pl.pallas_call(
        _cmp_kernel,
        grid=(b, g),
        in_specs=[pl.BlockSpec((1, 2, 1, n, width), lambda i, j: (i, 0, j, 0, 0)),
                  pl.BlockSpec(w1.shape, lambda i, j: (0, 0, 0, 0)),
                  pl.BlockSpec(w2.shape, lambda i, j: (0, 0, 0)),
                  pl.BlockSpec(pe.shape, lambda i, j: (0, 0, 0, 0)),
                  pl.BlockSpec((1, n, HEAD_DIM), lambda i, j: (i, 0, 0)),
                  pl.BlockSpec((1, n, HEAD_DIM), lambda i, j: (i, 0, 0))],
        out_specs=[pl.BlockSpec((1, 1, n, HEAD_DIM), lambda i, j: (i, j, 0, 0)),
                   pl.BlockSpec((1, 1, HEAD_DIM, n), lambda i, j: (i, j, 0, 0))],
        out_shape=[jax.ShapeDtypeStruct((b, g, n, HEAD_DIM), BF16),
                   jax.ShapeDtypeStruct((b, g, HEAD_DIM, n), BF16)],
        compiler_params=_params(("parallel", "parallel")),
        name="compress",
    )(kcvc, w1, w2, pe, cos_c, sin_c)


def _attn_kernel(q_ref, kc_ref, vct_ref, ks_ref, vst_ref, kw_ref, vwt_ref, gd_ref, o_ref,
                 sc_ref, imp_ref, sel_ref, acc_ref, gt_ref, sa_ref, sb_ref, qp_ref):
    n_cmp = kc_ref.shape[2]
    n_blk = imp_ref.shape[0] // (SLC_BLOCK // CMP_STRIDE)
    g = pl.program_id(1)
    qi = pl.program_id(2)
    t0 = qi * Q_BLOCK
    lanes_all = GROUP_WIDTH

    qt = jnp.concatenate([q_ref[0, r] for r in range(HEADS_PER_GROUP)], axis=1)
    t_lane = t0 + lax.broadcasted_iota(jnp.int32, (1, Q_BLOCK), 1)

    def tile_heads(x):
        return jnp.concatenate([x] * HEADS_PER_GROUP, axis=1)

    neg_row = jnp.full((1, lanes_all), NEG_INF, F32)
    zero_row = jnp.zeros((1, lanes_all), F32)

    chunk = min(CMP_CHUNK, n_cmp)
    n_c = (8 * qi + 6) // chunk + 1
    imp_ref[...] = jnp.zeros_like(imp_ref)
    acc_ref[0] = jnp.zeros((HEAD_DIM, lanes_all), F32)

    def cmp_scores(c, carry):
        m, l = carry
        off = pl.multiple_of(c * chunk, chunk)
        ci = off + lax.broadcasted_iota(jnp.int32, (chunk, Q_BLOCK), 0)
        vis = ci * CMP_STRIDE + (CMP_BLOCK - 1) <= t_lane
        s = _dot(kc_ref[0, 0, pl.ds(off, chunk), :], qt) + tile_heads(jnp.where(vis, 0.0, NEG_INF))
        sc_ref[pl.ds(off, chunk), :] = s
        m_new = jnp.maximum(m, jnp.max(s, axis=0, keepdims=True))
        l = l * jnp.exp2(m - m_new) + jnp.sum(jnp.exp2(s - m_new), axis=0, keepdims=True)
        return m_new, l

    m_c, l_c = lax.fori_loop(0, n_c, cmp_scores, (neg_row, zero_row))
    w_c = jnp.where(m_c > 0.5 * NEG_INF, 1.0 / l_c, 0.0)

    def cmp_out(c, carry):
        off = pl.multiple_of(c * chunk, chunk)
        pn = jnp.exp2(sc_ref[pl.ds(off, chunk), :] - m_c) * w_c
        imp = pn[:, 0:Q_BLOCK]
        for r in range(1, HEADS_PER_GROUP):
            imp = imp + pn[:, Q_BLOCK * r:Q_BLOCK * (r + 1)]
        imp_ref[pl.ds(off, chunk), :] = imp
        acc_ref[0] += _dot(vct_ref[0, 0, :, pl.ds(off, chunk)], pn.astype(BF16))
        return carry

    lax.fori_loop(0, n_c, cmp_out, 0)

    ratio = SLC_BLOCK // CMP_STRIDE
    score_in = pltpu.roll(imp_ref[pl.ds(ratio - 1, n_blk, stride=ratio), :], 1, 0)
    for m in range(ratio):
        score_in = score_in + imp_ref[pl.ds(m, n_blk, stride=ratio), :]
    blk = lax.broadcasted_iota(jnp.int32, (n_blk, Q_BLOCK), 0)
    cur = t_lane // SLC_BLOCK
    forced = (blk == 0) | (blk == cur) | (blk == cur - 1)
    visible = blk <= cur
    score0 = jnp.where(forced, -jnp.inf, jnp.where(visible, score_in, -SEL_FORCED))
    blk_f = blk.astype(F32)

    def pick(_, carry):
        score, sel = carry
        mx = jnp.max(score, axis=0, keepdims=True)
        first = jnp.min(jnp.where(score == mx, blk_f, float(n_blk)), axis=0, keepdims=True)
        hit = blk_f == first
        return jnp.where(hit, -jnp.inf, score), jnp.where(hit, 1.0, sel)

    _, sel = lax.fori_loop(0, min(N_SELECT, n_blk) - 3, pick, (score0, jnp.where(forced, 1.0, 0.0)))
    sel_ref[...] = jnp.zeros_like(sel_ref)
    sel_ref[0:n_blk, :] = jnp.where(visible, sel, 0.0)

    for h in range(qp_ref.shape[0]):
        picked = sel_ref[BLOCK_LANES * h:BLOCK_LANES * (h + 1), :]
        qp_ref[h, 0:HEAD_DIM, :] = qt
        qp_ref[h, HEAD_DIM:HEAD_DIM + BLOCK_LANES, :] = tile_heads(
            jnp.where(picked > 0.5, 0.0, NEG_INF).astype(BF16))

    def finish_scores(s, s_ref, n):
        s_ref[0:n, :] = s
        return jnp.max(s, axis=0, keepdims=True)

    def causal_bias(off, n):
        kpos = off + lax.broadcasted_iota(jnp.int32, (n, Q_BLOCK), 0)
        return tile_heads(jnp.where(kpos <= t_lane, 0.0, NEG_INF))

    def sel_scores(off, s_ref, causal):
        s = _dot(ks_ref[0, 0, pl.ds(off, SEL_CHUNK), :], qp_ref[off // BLOCK_SPAN])
        if causal:
            s = s + causal_bias(off, SEL_CHUNK)
        return finish_scores(s, s_ref, SEL_CHUNK)

    def win_scores(off, s_ref, n):
        kpos = off + lax.broadcasted_iota(jnp.int32, (n, Q_BLOCK), 0)
        in_win = (kpos <= t_lane) & (kpos > t_lane - WINDOW)
        s = _dot(kw_ref[0, 0, pl.ds(off, n), :], qt) + tile_heads(jnp.where(in_win, 0.0, NEG_INF))
        return finish_scores(s, s_ref, n)

    def value_tile(vt, s_ref, n, tile_max, m, l, branch):
        m_new = jnp.maximum(m, tile_max)
        alpha = jnp.exp2(m - m_new)
        p = jnp.exp2(s_ref[0:n, :] - m_new)
        acc_ref[branch] = alpha * acc_ref[branch] + _dot(vt, p.astype(BF16))
        return m_new, alpha * l + jnp.sum(p, axis=0, keepdims=True)

    def sel_values(off, s_ref, tile_max, m, l):
        return value_tile(vst_ref[0, 0, :, pl.ds(off, SEL_CHUNK)], s_ref, SEL_CHUNK, tile_max, m, l, 1)

    trip = 2 * SEL_CHUNK
    n_k = (t0 + Q_BLOCK - 1) // trip + 1
    acc_ref[1] = jnp.zeros((HEAD_DIM, lanes_all), F32)

    def sel_step(c, carry):
        m, l, max_a = carry
        off_a = pl.multiple_of(c * trip, trip)
        off_b = pl.multiple_of(off_a + SEL_CHUNK, SEL_CHUNK)
        max_b = sel_scores(off_b, sb_ref, False)
        m, l = sel_values(off_a, sa_ref, max_a, m, l)
        max_a = sel_scores(pl.multiple_of(off_a + trip, trip), sa_ref, False)
        m, l = sel_values(off_b, sb_ref, max_b, m, l)
        return m, l, max_a

    max_a0 = sel_scores(0, sa_ref, False)
    n_main = n_k - 1
    carry = lax.fori_loop(0, n_main // 2, lambda i, cr: sel_step(2 * i + 1, sel_step(2 * i, cr)),
                          (neg_row, zero_row, max_a0))
    m_s, l_s, _ = lax.fori_loop(n_main - n_main % 2, n_main, sel_step, carry)

    off_a = pl.multiple_of((n_k - 1) * trip, trip)
    off_b = pl.multiple_of(off_a + SEL_CHUNK, SEL_CHUNK)
    ws = pl.multiple_of(jnp.maximum(t0 - WINDOW, 0), Q_BLOCK)
    max_a = sel_scores(off_a, sa_ref, True)
    max_b = sel_scores(off_b, sb_ref, True)
    m_s, l_s = sel_values(off_a, sa_ref, max_a, m_s, l_s)
    max_w0 = win_scores(ws, sa_ref, SEL_CHUNK)
    m_s, l_s = sel_values(off_b, sb_ref, max_b, m_s, l_s)
    inv_s = 1.0 / l_s

    acc_ref[2] = jnp.zeros((HEAD_DIM, lanes_all), F32)
    last_n = WIN_SPAN - 2 * SEL_CHUNK
    ws1 = pl.multiple_of(ws + SEL_CHUNK, Q_BLOCK)
    ws2 = pl.multiple_of(ws + 2 * SEL_CHUNK, Q_BLOCK)
    max_w1 = win_scores(ws1, sb_ref, SEL_CHUNK)
    m_w, l_w = value_tile(vwt_ref[0, 0, :, pl.ds(ws, SEL_CHUNK)], sa_ref, SEL_CHUNK, max_w0, neg_row, zero_row, 2)
    max_w2 = win_scores(ws2, sa_ref, last_n)
    m_w, l_w = value_tile(vwt_ref[0, 0, :, pl.ds(ws1, SEL_CHUNK)], sb_ref, SEL_CHUNK, max_w1, m_w, l_w, 2)
    m_w, l_w = value_tile(vwt_ref[0, 0, :, pl.ds(ws2, last_n)], sa_ref, last_n, max_w2, m_w, l_w, 2)
    inv_w = 1.0 / l_w

    gt_ref[...] = _sigmoid(gd_ref[...]).T
    for r in range(HEADS_PER_GROUP):
        sl = slice(Q_BLOCK * r, Q_BLOCK * (r + 1))
        base = 3 * (HEADS_PER_GROUP * g + r)
        o_t = (gt_ref[pl.ds(base, 1), :] * acc_ref[0, :, sl]
               + gt_ref[pl.ds(base + 1, 1), :] * (acc_ref[1, :, sl] * inv_s[:, sl])
               + gt_ref[pl.ds(base + 2, 1), :] * (acc_ref[2, :, sl] * inv_w[:, sl]))
        o_ref[0, :, HEAD_DIM * r:HEAD_DIM * (r + 1)] = o_t.T


def _attention(q_r, k_cmp, v_cmp_t, k_slc, v_slc_t, k_win, v_win_t, proj, b, s):
    g = N_KV_GROUPS
    nqb = s // Q_BLOCK
    n_cmp = s // CMP_STRIDE
    n_blk = s // SLC_BLOCK
    n_bias = pl.cdiv(n_blk, BLOCK_LANES)
    once = pl.Buffered(1)
    kspec = lambda n, w: pl.BlockSpec((1, 1, n, w), lambda i, j, k: (i, j, 0, 0), pipeline_mode=once)
    vspec = lambda n: pl.BlockSpec((1, 1, HEAD_DIM, n), lambda i, j, k: (i, j, 0, 0), pipeline_mode=once)
    return pl.pallas_call(
        _attn_kernel,
        grid=(b, g, nqb),
        in_specs=[pl.BlockSpec((1, HEADS_PER_GROUP, HEAD_DIM, Q_BLOCK), lambda i, j, k: (i, j, 0, k)),
                  kspec(n_cmp, HEAD_DIM), vspec(n_cmp), kspec(s, HEAD_DIM + BLOCK_LANES), vspec(s),
                  kspec(s, HEAD_DIM), vspec(s),
                  pl.BlockSpec((Q_BLOCK, 128), lambda i, j, k: (i * nqb + k, GD_BLOCK))],
        out_specs=pl.BlockSpec((1, Q_BLOCK, GROUP_WIDTH), lambda i, j, k: (i, k, j)),
        out_shape=jax.ShapeDtypeStruct((b, s, D_MODEL), F32),
        scratch_shapes=[pltpu.VMEM((n_cmp, GROUP_WIDTH), F32),
                        pltpu.VMEM((n_cmp, Q_BLOCK), F32),
                        pltpu.VMEM((n_bias * BLOCK_LANES, Q_BLOCK), F32),
                        pltpu.VMEM((3, HEAD_DIM, GROUP_WIDTH), F32),
                        pltpu.VMEM((128, Q_BLOCK), F32),
                        pltpu.VMEM((SEL_CHUNK, GROUP_WIDTH), F32),
                        pltpu.VMEM((SEL_CHUNK, GROUP_WIDTH), F32),
                        pltpu.VMEM((n_bias, HEAD_DIM + BLOCK_LANES, GROUP_WIDTH), BF16)],
        compiler_params=_params(("parallel", "parallel", "arbitrary")),
        name="nsa_attention",
    )(q_r, k_cmp, v_cmp_t, k_slc, v_slc_t, k_win, v_win_t, proj)


def _ssm_kernel(xbc_ref, gd_ref, z_ref, cw_ref, cb_ref, dtb_ref, alog_ref, dsk_ref, nw_ref, tri_ref,
                y_ref, tail_ref, st_ref, u_ref):
    L = SSD_CHUNK

    @pl.when(pl.program_id(1) == 0)
    def _():
        tail_ref[...] = jnp.zeros_like(tail_ref)
        st_ref[...] = jnp.zeros_like(st_ref)

    strip = 512
    for cs in range(CONV_DIM // strip):
        sl = slice(strip * cs, strip * (cs + 1))
        cur = xbc_ref[:, sl]
        ext = jnp.concatenate([tail_ref[:, sl], cur], axis=0)
        acc = cb_ref[:, sl] + cw_ref[0:1, sl] * ext[5:5 + L]
        for k in range(1, CONV_WIDTH):
            acc = acc + cw_ref[k:k + 1, sl] * ext[5 + k:5 + k + L]
        tail_ref[:, sl] = cur[L - 8:L]
        u_ref[:, sl] = acc * _sigmoid(acc)

    x_dt = gd_ref[...] + dtb_ref[...]
    dt = jnp.maximum(x_dt, 0.0) + jnp.log1p(jnp.exp(-jnp.abs(x_dt)))
    da = dt * -jnp.exp(alog_ref[...])
    tri = tri_ref[...]
    cum = jnp.zeros((L, 128), F32)
    for part in _split3(da):
        cum = cum + _dot(tri, part)
    cum_t = cum.T
    row = lax.broadcasted_iota(jnp.int32, (L, L), 0)
    col = lax.broadcasted_iota(jnp.int32, (L, L), 1)
    causal = row >= col

    last = cum[L - 1:L, :]
    per_head = jnp.concatenate(
        [dt, dt * jnp.exp(last - cum), jnp.exp(cum), jnp.broadcast_to(jnp.exp(last), (8, 128))], axis=0)
    per_head_parts = _split3(per_head)
    gw = SSM_HEADS_PER_GROUP * SSM_HEAD_DIM
    src_lane = lax.broadcasted_iota(jnp.int32, (128, gw), 0)
    head_of_lane = lax.broadcasted_iota(jnp.int32, (128, gw), 1) // SSM_HEAD_DIM
    out_head = lax.broadcasted_iota(jnp.int32, (L, gw), 1) // SSM_HEAD_DIM

    for g in range(N_SSM_GROUPS):
        b_g = u_ref[:, D_SSM + D_STATE * g:D_SSM + D_STATE * (g + 1)]
        c_g = u_ref[:, D_SSM + N_SSM_GROUPS * D_STATE + D_STATE * g:
                    D_SSM + N_SSM_GROUPS * D_STATE + D_STATE * (g + 1)]
        b_b = b_g.astype(BF16)
        c_b = c_g.astype(BF16)
        cb = _dot_nt(c_b, b_b)
        b_t = b_g.T.astype(BF16)
        st = st_ref[g]
        sl = slice(gw * g, gw * (g + 1))
        x_g = u_ref[:, sl]

        spread = jnp.where(src_lane == DT_LANE0 + SSM_HEADS_PER_GROUP * g + head_of_lane, 1.0, 0.0).astype(BF16)
        cols = _dot(per_head_parts[0], spread)
        for part in per_head_parts[1:]:
            cols = cols + _dot(part, spread)
        xdt = x_g * cols[0:L]
        x_end = x_g * cols[L:2 * L]
        y_g = _dot(c_b, st.astype(BF16)) * cols[2 * L:3 * L] + dsk_ref[:, sl] * x_g
        for r in range(SSM_HEADS_PER_GROUP):
            lane = DT_LANE0 + SSM_HEADS_PER_GROUP * g + r
            cc = cum[:, lane:lane + 1]
            cr = cum_t[lane:lane + 1, :]
            decay = jnp.exp(jnp.where(causal, cc - cr, -jnp.inf))
            w = (cb * decay).astype(BF16)
            y_g = y_g + _dot(w, jnp.where(out_head == r, xdt, 0.0).astype(BF16))
        st_ref[g] = st * cols[3 * L:3 * L + 1] + _dot(b_t, x_end.astype(BF16))
        z_g = z_ref[:, sl]
        v = y_g * (z_g * _sigmoid(z_g))
        ms = jnp.mean(v * v, axis=-1, keepdims=True)
        y_ref[:, sl] = (v * lax.rsqrt(ms + EPS) * nw_ref[:, sl]).astype(BF16)


def _ssm(proj, conv_w, conv_b, dtb, alog, dsk, norm_w, tri, b, s):
    nc = s // SSD_CHUNK
    full = lambda shape: pl.BlockSpec(shape, lambda i, j: (0,) * len(shape))
    return pl.pallas_call(
        _ssm_kernel,
        grid=(b, nc),
        in_specs=[pl.BlockSpec((SSD_CHUNK, CONV_DIM), lambda i, j: (i * nc + j, 1)),
                  pl.BlockSpec((SSD_CHUNK, 128), lambda i, j: (i * nc + j, GD_BLOCK)),
                  pl.BlockSpec((SSD_CHUNK, D_SSM), lambda i, j: (i * nc + j, 5)),
                  full((CONV_WIDTH, CONV_DIM)), full((1, CONV_DIM)),
                  full((1, 128)), full((1, 128)), full((1, D_SSM)),
                  full((1, D_SSM)), full((SSD_CHUNK, SSD_CHUNK))],
        out_specs=pl.BlockSpec((SSD_CHUNK, D_SSM), lambda i, j: (i * nc + j, 0)),
        out_shape=jax.ShapeDtypeStruct((b * s, D_SSM), BF16),
        scratch_shapes=[pltpu.VMEM((8, CONV_DIM), F32),
                        pltpu.VMEM((N_SSM_GROUPS, D_STATE, SSM_HEADS_PER_GROUP * SSM_HEAD_DIM), F32),
                        pltpu.VMEM((SSD_CHUNK, CONV_DIM), F32)],
        compiler_params=_params(("parallel", "arbitrary")),
        name="ssd_scan",
    )(proj, proj, proj, conv_w, conv_b, dtb, alog, dsk, norm_w, tri)


def _out_kernel(o_ref, z_ref, ys_ref, x_ref, naw_ref, w_ref, fw_ref, out_ref, *, final):
    z = z_ref[...]
    v = o_ref[...] * (z * _sigmoid(z))
    ms = jnp.mean(v * v, axis=-1, keepdims=True)
    ya = (v * lax.rsqrt(ms + EPS) * naw_ref[...]).astype(BF16)
    acc = x_ref[...] + _dot(ya, w_ref[0:D_MODEL, :]) + _dot(ys_ref[...], w_ref[D_MODEL:2 * D_MODEL, :])
    if final:
        ms2 = jnp.mean(acc * acc, axis=-1, keepdims=True)
        acc = acc * lax.rsqrt(ms2 + EPS) * fw_ref[...]
    out_ref[...] = acc


def _out_proj(o_attn, proj, y_ssm, x2d, norm_attn_w, w_out, final_w, final):
    t = x2d.shape[0]
    rowblk = lambda c: pl.BlockSpec((OUT_TM, D_MODEL), lambda i: (i, c))
    vec = pl.BlockSpec((1, D_MODEL), lambda i: (0, 0))
    return pl.pallas_call(
        functools.partial(_out_kernel, final=final),
        grid=(t // OUT_TM,),
        in_specs=[rowblk(0), rowblk(4), rowblk(0), rowblk(0), vec,
                  pl.BlockSpec((2 * D_MODEL, D_MODEL), lambda i: (0, 0), pipeline_mode=pl.Buffered(1)),
                  vec],
        out_specs=rowblk(0),
        out_shape=jax.ShapeDtypeStruct((t, D_MODEL), F32),
        compiler_params=_params(("parallel",)),
        name="out_proj",
    )(o_attn, proj, y_ssm, x2d, norm_attn_w, w_out, final_w)


def _pack_w_in(w):
    q0, kv0, gate0, za0, xbc0, dt0, zs0, end = 0, 2048, 3584, 3632, 5680, 9776, 9808, 11856
    pad = jnp.zeros((w.shape[0], 2048 - KV_COLS - GATE_LANES - N_HEADS_SSM), w.dtype)
    return jnp.concatenate(
        [w[:, kv0:gate0], w[:, gate0:za0], w[:, dt0:zs0], pad,
         w[:, q0:kv0], w[:, xbc0:dt0], w[:, za0:xbc0], w[:, zs0:end]], axis=1).astype(BF16)


def _dt_lanes(v):
    return jnp.zeros((1, 128), F32).at[0, DT_LANE0:DT_LANE0 + N_HEADS_SSM].set(v.astype(F32))


def kernel(x, positions, norm_w, w_in, cmp_pe_k, cmp_w1_k, cmp_w2_k, cmp_pe_v, cmp_w1_v, cmp_w2_v,
           conv_w, conv_b, dt_bias, a_log, d_skip, norm_attn_w, norm_ssm_w, w_out, final_norm_w):
    b, s, d = x.shape
    depth = w_in.shape[0]
    assert d == D_MODEL and s % 2048 == 0
    n_chunks = s // CMP_STRIDE

    half = HEAD_DIM // 2
    inv_freq = ROPE_THETA ** (-jnp.arange(half, dtype=F32) / half)
    inv_full = jnp.concatenate([inv_freq, inv_freq])[None, :]
    pos_f = positions.astype(F32)
    cos_t, sin_t = _rope_tables(pos_f[:, :, None], inv_full)
    pos_c = jnp.concatenate([pos_f[:, CMP_BLOCK - 1::CMP_STRIDE], pos_f[:, -1:]], axis=1)
    cos_c, sin_c = _rope_tables(pos_c[:, :, None], inv_full)

    tri = jnp.tril(jnp.ones((SSD_CHUNK, SSD_CHUNK), F32)).astype(BF16)
    x2d = x.reshape(b * s, d)
    for l in range(depth):
        w_pack = _pack_w_in(w_in[l])
        w1 = jnp.stack([cmp_w1_k[l], cmp_w1_v[l]]).reshape(2, 2, CMP_STRIDE * HEAD_DIM, CMP_HIDDEN).astype(BF16)
        w2 = jnp.stack([cmp_w2_k[l], cmp_w2_v[l]]).astype(BF16)
        pe = jnp.stack([cmp_pe_k[l], cmp_pe_v[l]]).reshape(2, 2, 1, CMP_STRIDE * HEAD_DIM)

        proj = _in_proj(x2d, norm_w[l][None, :], w_pack)
        q_r, kcvc, k_slc, k_win, v_slc_t, v_win_t = _prep(proj, cos_t, sin_t, b, s)
        kcvc = kcvc.reshape(b, 2, N_KV_GROUPS, n_chunks, CMP_STRIDE * HEAD_DIM)
        k_cmp, v_cmp_t = _compress(kcvc, w1, w2, pe, cos_c, sin_c)
        o_attn = _attention(q_r, k_cmp, v_cmp_t, k_slc, v_slc_t, k_win, v_win_t, proj, b, s)
        y_ssm = _ssm(proj, conv_w[l], conv_b[l][None, :], _dt_lanes(dt_bias[l]),
                     _dt_lanes(a_log[l]), jnp.repeat(d_skip[l].astype(F32), SSM_HEAD_DIM)[None, :],
                     norm_ssm_w[l][None, :], tri, b, s)
        x2d = _out_proj(o_attn.reshape(b * s, d), proj, y_ssm, x2d, norm_attn_w[l][None, :],
                        w_out[l].astype(BF16), final_norm_w[None, :], final=(l == depth - 1))
    return x2d.reshape(b, s, d)
```

```python
import functools

import jax
import jax.numpy as jnp
from jax import lax
from jax.experimental import pallas as pl
from jax.experimental.pallas import tpu as pltpu

F32 = jnp.float32
BF16 = jnp.bfloat16

D_MODEL = 2048
HEAD_DIM = 128
N_HEADS_ATTN = 16
N_KV_GROUPS = 2
HEADS_PER_GROUP = 8
GROUP_WIDTH = HEADS_PER_GROUP * HEAD_DIM
CMP_STRIDE = 16
CMP_BLOCK = 32
CMP_HIDDEN = 256
SLC_BLOCK = 64
N_SELECT = 16
WINDOW = 512
Q_BLOCK = 128
ROPE_THETA = 10000.0
NEG_INF = -1e30
SEL_FORCED = 1e4
N_HEADS_SSM = 32
SSM_HEAD_DIM = 64
N_SSM_GROUPS = 8
SSM_HEADS_PER_GROUP = 4
D_STATE = 128
D_SSM = 2048
CONV_WIDTH = 4
CONV_DIM = 4096
SSD_CHUNK = 256
EPS = 1e-6
Q_SCALE = HEAD_DIM ** -0.5 * 1.4426950408889634

KV_COLS = 6 * N_KV_GROUPS * HEAD_DIM
GD_BLOCK = KV_COLS // 128
GATE_LANES = 3 * N_HEADS_ATTN
DT_LANE0 = GATE_LANES
D_PACK = 12288
PROJ_TM = 512
PROJ_TN = 2048
PREP_TQ = 256
OUT_TM = 256
CMP_CHUNK = 256
SEL_CHUNK = 256
BLOCK_LANES = 128
BLOCK_SPAN = BLOCK_LANES * SLC_BLOCK
WIN_SPAN = WINDOW + Q_BLOCK
VMEM_LIMIT = 56 * 1024 * 1024


def _dot(a, b):
    return jnp.dot(a, b, preferred_element_type=F32)


def _dot_nt(a, b):
    return lax.dot_general(a, b, (((1,), (1,)), ((), ())), preferred_element_type=F32)


def _split3(x):
    hi = x.astype(BF16)
    r1 = x - hi.astype(F32)
    mid = r1.astype(BF16)
    lo = (r1 - mid.astype(F32)).astype(BF16)
    return hi, mid, lo


def _sigmoid(x):
    return 1.0 / (1.0 + jnp.exp(-x))


def _params(sem):
    return pltpu.CompilerParams(dimension_semantics=sem, vmem_limit_bytes=VMEM_LIMIT)


def _rope_table_kernel(pos_ref, inv_ref, cos_ref, sin_ref):
    ang = pos_ref[0] * inv_ref[...]
    lane = lax.broadcasted_iota(jnp.int32, ang.shape, 1)
    cos_ref[0] = jnp.cos(ang)
    s = jnp.sin(ang)
    sin_ref[0] = jnp.where(lane < HEAD_DIM // 2, -s, s)


def _rope_tables(pos_f, inv_full):
    b, n, _ = pos_f.shape
    rows = n if n <= 2048 else 2048
    out = jax.ShapeDtypeStruct((b, n, HEAD_DIM), F32)
    return pl.pallas_call(
        _rope_table_kernel,
        grid=(b, n // rows),
        in_specs=[pl.BlockSpec((1, rows, 1), lambda i, j: (i, j, 0)),
                  pl.BlockSpec((1, HEAD_DIM), lambda i, j: (0, 0))],
        out_specs=[pl.BlockSpec((1, rows, HEAD_DIM), lambda i, j: (i, j, 0))] * 2,
        out_shape=[out, out],
        compiler_params=_params(("parallel", "parallel")),
        name="rope_tables",
    )(pos_f, inv_full)


def _proj_kernel(x_ref, nw_ref, w_ref, o_ref, h_ref):
    @pl.when(pl.program_id(1) == 0)
    def _():
        x = x_ref[...]
        ms = jnp.mean(x * x, axis=-1, keepdims=True)
        h_ref[...] = (x * lax.rsqrt(ms + EPS) * nw_ref[...]).astype(BF16)

    o_ref[...] = _dot(h_ref[...], w_ref[...])


def _in_proj(x2d, norm_w, w_pack):
    t = x2d.shape[0]
    return pl.pallas_call(
        _proj_kernel,
        grid=(t // PROJ_TM, D_PACK // PROJ_TN),
        in_specs=[pl.BlockSpec((PROJ_TM, D_MODEL), lambda i, j: (i, 0)),
                  pl.BlockSpec((1, D_MODEL), lambda i, j: (0, 0)),
                  pl.BlockSpec((D_MODEL, PROJ_TN), lambda i, j: (0, j))],
        out_specs=pl.BlockSpec((PROJ_TM, PROJ_TN), lambda i, j: (i, j)),
        out_shape=jax.ShapeDtypeStruct((t, D_PACK), F32),
        scratch_shapes=[pltpu.VMEM((PROJ_TM, D_MODEL), BF16)],
        compiler_params=_params(("parallel", "arbitrary")),
        name="in_proj",
    )(x2d, norm_w, w_pack)


def _prep_kernel(kv_ref, q_ref, cos_ref, sin_ref,
                 qo_ref, kcvc_ref, ks_ref, kw_ref, vst_ref, vwt_ref):
    c = cos_ref[0]
    s = sin_ref[0]

    def rope(u):
        return u * c + pltpu.roll(u, HEAD_DIM // 2, 1) * s

    for h in range(N_HEADS_ATTN):
        sl = slice(HEAD_DIM * h, HEAD_DIM * (h + 1))
        qo_ref[0, h] = (rope(q_ref[:, sl]) * Q_SCALE).T.astype(BF16)

    def seg(i, g):
        lo = (i * N_KV_GROUPS + g) * HEAD_DIM
        return kv_ref[:, lo:lo + HEAD_DIM]

    tq = q_ref.shape[0]
    tok = pl.program_id(1) * tq + lax.broadcasted_iota(jnp.int32, (tq, BLOCK_LANES), 0)
    lane = lax.broadcasted_iota(jnp.int32, (tq, BLOCK_LANES), 1)
    block_onehot = jnp.where((tok // SLC_BLOCK) % BLOCK_LANES == lane, 1.0, 0.0).astype(BF16)

    for g in range(N_KV_GROUPS):
        kcvc_ref[0, 0, g] = seg(0, g).astype(BF16)
        kcvc_ref[0, 1, g] = seg(1, g).astype(BF16)
        ks_ref[0, g, :, 0:HEAD_DIM] = rope(seg(2, g)).astype(BF16)
        ks_ref[0, g, :, HEAD_DIM:HEAD_DIM + BLOCK_LANES] = block_onehot
        vst_ref[0, g] = seg(3, g).T.astype(BF16)
        kw_ref[0, g] = rope(seg(4, g)).astype(BF16)
        vwt_ref[0, g] = seg(5, g).T.astype(BF16)


def _prep(proj, cos_t, sin_t, b, s):
    nq = s // PREP_TQ
    g = N_KV_GROUPS
    row = lambda i, j: (i * nq + j, 0)
    kshape = jax.ShapeDtypeStruct((b, g, s, HEAD_DIM), BF16)
    vshape = jax.ShapeDtypeStruct((b, g, HEAD_DIM, s), BF16)
    kspec = pl.BlockSpec((1, g, PREP_TQ, HEAD_DIM), lambda i, j: (i, 0, j, 0))
    vspec = pl.BlockSpec((1, g, HEAD_DIM, PREP_TQ), lambda i, j: (i, 0, 0, j))
    ks_width = HEAD_DIM + BLOCK_LANES
    ks_shape = jax.ShapeDtypeStruct((b, g, s, ks_width), BF16)
    ks_spec = pl.BlockSpec((1, g, PREP_TQ, ks_width), lambda i, j: (i, 0, j, 0))
    return pl.pallas_call(
        _prep_kernel,
        grid=(b, nq),
        in_specs=[pl.BlockSpec((PREP_TQ, KV_COLS), row),
                  pl.BlockSpec((PREP_TQ, D_MODEL), lambda i, j: (i * nq + j, 1)),
                  pl.BlockSpec((1, PREP_TQ, HEAD_DIM), lambda i, j: (i, j, 0)),
                  pl.BlockSpec((1, PREP_TQ, HEAD_DIM), lambda i, j: (i, j, 0))],
        out_specs=[pl.BlockSpec((1, N_HEADS_ATTN, HEAD_DIM, PREP_TQ), lambda i, j: (i, 0, 0, j)),
                   pl.BlockSpec((1, 2, g, PREP_TQ, HEAD_DIM), lambda i, j: (i, 0, 0, j, 0)),
                   ks_spec, kspec, vspec, vspec],
        out_shape=[jax.ShapeDtypeStruct((b, N_HEADS_ATTN, HEAD_DIM, s), BF16),
                   jax.ShapeDtypeStruct((b, 2, g, s, HEAD_DIM), BF16),
                   ks_shape, kshape, vshape, vshape],
        compiler_params=_params(("parallel", "parallel")),
        name="rope_relayout",
    )(proj, proj, cos_t, sin_t)


def _cmp_kernel(x_ref, w1_ref, w2_ref, pe_ref, cos_ref, sin_ref, kc_ref, vct_ref):
    n = x_ref.shape[3]
    for kv in range(2):
        x = x_ref[0, kv, 0]
        a0 = _dot(x, w1_ref[kv, 0])
        a1 = _dot(x, w1_ref[kv, 1])
        bias = jnp.zeros((8, CMP_HIDDEN), F32)
        for r in range(2):
            pe = jnp.broadcast_to(pe_ref[kv, r], (8, CMP_STRIDE * HEAD_DIM))
            for part in _split3(pe):
                bias = bias + _dot(part, w1_ref[kv, r])
        hid = a0 + pltpu.roll(a1, n - 1, 0) + bias[0:1]
        act = hid * _sigmoid(hid)
        out = _dot(act.astype(BF16), w2_ref[kv])
        if kv == 0:
            roped = out * cos_ref[0] + pltpu.roll(out, HEAD_DIM // 2, 1) * sin_ref[0]
            kc_ref[0, 0] = roped.astype(BF16)
        else:
            vct_ref[0, 0] = out.T.astype(BF16)


def _compress(kcvc, w1, w2, pe, cos_c, sin_c):
    b, _, g, n, width = kcvc.shape
    return pl.pallas_call(
        _cmp_kernel,
        grid=(b, g),
        in_specs=[pl.BlockSpec((1, 2, 1, n, width), lambda i, j: (i, 0, j, 0, 0)),
                  pl.BlockSpec(w1.shape, lambda i, j: (0, 0, 0, 0)),
                  pl.BlockSpec(w2.shape, lambda i, j: (0, 0, 0)),
                  pl.BlockSpec(pe.shape, lambda i, j: (0, 0, 0, 0)),
                  pl.BlockSpec((1, n, HEAD_DIM), lambda i, j: (i, 0, 0)),
                  pl.BlockSpec((1, n, HEAD_DIM), lambda i, j: (i, 0, 0))],
        out_specs=[pl.BlockSpec((1, 1, n, HEAD_DIM), lambda i, j: (i, j, 0, 0)),
                   pl.BlockSpec((1, 1, HEAD_DIM, n), lambda i, j: (i, j, 0, 0))],
        out_shape=[jax.ShapeDtypeStruct((b, g, n, HEAD_DIM), BF16),
                   jax.ShapeDtypeStruct((b, g, HEAD_DIM, n), BF16)],
        compiler_params=_params(("parallel", "parallel")),
        name="compress",
    )(kcvc, w1, w2, pe, cos_c, sin_c)


def _attn_kernel(q_ref, kc_ref, vct_ref, ks_ref, vst_ref, kw_ref, vwt_ref, gd_ref, o_ref,
                 sc_ref, imp_ref, sel_ref, acc_ref, gt_ref, sa_ref, sb_ref, qp_ref):
    n_cmp = kc_ref.shape[2]
    n_blk = imp_ref.shape[0] // (SLC_BLOCK // CMP_STRIDE)
    g = pl.program_id(1)
    qi = pl.program_id(2)
    t0 = qi * Q_BLOCK
    lanes_all = GROUP_WIDTH

    qt = jnp.concatenate([q_ref[0, r] for r in range(HEADS_PER_GROUP)], axis=1)
    t_lane = t0 + lax.broadcasted_iota(jnp.int32, (1, Q_BLOCK), 1)

    def tile_heads(x):
        return jnp.concatenate([x] * HEADS_PER_GROUP, axis=1)

    neg_row = jnp.full((1, lanes_all), NEG_INF, F32)
    zero_row = jnp.zeros((1, lanes_all), F32)

    chunk = min(CMP_CHUNK, n_cmp)
    n_c = (8 * qi + 6) // chunk + 1
    imp_ref[...] = jnp.zeros_like(imp_ref)
    acc_ref[0] = jnp.zeros((HEAD_DIM, lanes_all), F32)

    def cmp_scores(c, carry):
        m, l = carry
        off = pl.multiple_of(c * chunk, chunk)
        ci = off + lax.broadcasted_iota(jnp.int32, (chunk, Q_BLOCK), 0)
        vis = ci * CMP_STRIDE + (CMP_BLOCK - 1) <= t_lane
        s = _dot(kc_ref[0, 0, pl.ds(off, chunk), :], qt) + tile_heads(jnp.where(vis, 0.0, NEG_INF))
        sc_ref[pl.ds(off, chunk), :] = s
        m_new = jnp.maximum(m, jnp.max(s, axis=0, keepdims=True))
        l = l * jnp.exp2(m - m_new) + jnp.sum(jnp.exp2(s - m_new), axis=0, keepdims=True)
        return m_new, l

    def by_pairs(body, init):
        cr = lax.fori_loop(0, n_c // 2, lambda i, cr: body(2 * i + 1, body(2 * i, cr)), init)
        return lax.fori_loop(n_c - n_c % 2, n_c, body, cr)

    m_c, l_c = by_pairs(cmp_scores, (neg_row, zero_row))
    w_c = jnp.where(m_c > 0.5 * NEG_INF, 1.0 / l_c, 0.0)

    def cmp_out(c, carry):
        off = pl.multiple_of(c * chunk, chunk)
        pn = jnp.exp2(sc_ref[pl.ds(off, chunk), :] - m_c) * w_c
        imp = pn[:, 0:Q_BLOCK]
        for r in range(1, HEADS_PER_GROUP):
            imp = imp + pn[:, Q_BLOCK * r:Q_BLOCK * (r + 1)]
        imp_ref[pl.ds(off, chunk), :] = imp
        acc_ref[0] += _dot(vct_ref[0, 0, :, pl.ds(off, chunk)], pn.astype(BF16))
        return carry

    by_pairs(cmp_out, 0)

    ratio = SLC_BLOCK // CMP_STRIDE
    score_in = pltpu.roll(imp_ref[pl.ds(ratio - 1, n_blk, stride=ratio), :], 1, 0)
    for m in range(ratio):
        score_in = score_in + imp_ref[pl.ds(m, n_blk, stride=ratio), :]
    blk = lax.broadcasted_iota(jnp.int32, (n_blk, Q_BLOCK), 0)
    cur = t_lane // SLC_BLOCK
    forced = (blk == 0) | (blk == cur) | (blk == cur - 1)
    visible = blk <= cur
    score0 = jnp.where(forced, -jnp.inf, jnp.where(visible, score_in, -SEL_FORCED))
    blk_f = blk.astype(F32)

    def pick(_, carry):
        score, sel = carry
        mx = jnp.max(score, axis=0, keepdims=True)
        first = jnp.min(jnp.where(score == mx, blk_f, float(n_blk)), axis=0, keepdims=True)
        hit = blk_f == first
        return jnp.where(hit, -jnp.inf, score), jnp.where(hit, 1.0, sel)

    _, sel = lax.fori_loop(0, min(N_SELECT, n_blk) - 3, pick, (score0, jnp.where(forced, 1.0, 0.0)))
    sel_ref[...] = jnp.zeros_like(sel_ref)
    sel_ref[0:n_blk, :] = jnp.where(visible, sel, 0.0)

    for h in range(qp_ref.shape[0]):
        picked = sel_ref[BLOCK_LANES * h:BLOCK_LANES * (h + 1), :]
        qp_ref[h, 0:HEAD_DIM, :] = qt
        qp_ref[h, HEAD_DIM:HEAD_DIM + BLOCK_LANES, :] = tile_heads(
            jnp.where(picked > 0.5, 0.0, NEG_INF).astype(BF16))

    def finish_scores(s, s_ref, n):
        s_ref[0:n, :] = s
        return jnp.max(s, axis=0, keepdims=True)

    def causal_bias(off, n):
        kpos = off + lax.broadcasted_iota(jnp.int32, (n, Q_BLOCK), 0)
        return tile_heads(jnp.where(kpos <= t_lane, 0.0, NEG_INF))

    def sel_scores(off, s_ref, causal):
        s = _dot(ks_ref[0, 0, pl.ds(off, SEL_CHUNK), :], qp_ref[off // BLOCK_SPAN])
        if causal:
            s = s + causal_bias(off, SEL_CHUNK)
        return finish_scores(s, s_ref, SEL_CHUNK)

    def win_scores(off, s_ref, n):
        kpos = off + lax.broadcasted_iota(jnp.int32, (n, Q_BLOCK), 0)
        in_win = (kpos <= t_lane) & (kpos > t_lane - WINDOW)
        s = _dot(kw_ref[0, 0, pl.ds(off, n), :], qt) + tile_heads(jnp.where(in_win, 0.0, NEG_INF))
        return finish_scores(s, s_ref, n)

    def value_tile(vt, s_ref, n, tile_max, m, l, branch):
        m_new = jnp.maximum(m, tile_max)
        alpha = jnp.exp2(m - m_new)
        p = jnp.exp2(s_ref[0:n, :] - m_new)
        acc_ref[branch] = alpha * acc_ref[branch] + _dot(vt, p.astype(BF16))
        return m_new, alpha * l + jnp.sum(p, axis=0, keepdims=True)

    def sel_values(off, s_ref, tile_max, m, l):
        return value_tile(vst_ref[0, 0, :, pl.ds(off, SEL_CHUNK)], s_ref, SEL_CHUNK, tile_max, m, l, 1)

    trip = 2 * SEL_CHUNK
    n_k = (t0 + Q_BLOCK - 1) // trip + 1
    acc_ref[1] = jnp.zeros((HEAD_DIM, lanes_all), F32)

    def sel_step(c, carry):
        m, l, max_a = carry
        off_a = pl.multiple_of(c * trip, trip)
        off_b = pl.multiple_of(off_a + SEL_CHUNK, SEL_CHUNK)
        max_b = sel_scores(off_b, sb_ref, False)
        m, l = sel_values(off_a, sa_ref, max_a, m, l)
        max_a = sel_scores(pl.multiple_of(off_a + trip, trip), sa_ref, False)
        m, l = sel_values(off_b, sb_ref, max_b, m, l)
        return m, l, max_a

    max_a0 = sel_scores(0, sa_ref, False)
    n_main = n_k - 1
    def trips(first, count, cr):
        for j in range(count):
            cr = sel_step(first + j, cr)
        return cr

    n_quad = n_main // 4
    carry = lax.fori_loop(0, n_quad, lambda i, cr: trips(4 * i, 4, cr), (neg_row, zero_row, max_a0))
    n_pair = (n_main - 4 * n_quad) // 2
    carry = lax.fori_loop(0, n_pair, lambda i, cr: trips(4 * n_quad, 2, cr), carry)
    m_s, l_s, _ = lax.fori_loop(4 * n_quad + 2 * n_pair, n_main, sel_step, carry)

    off_a = pl.multiple_of((n_k - 1) * trip, trip)
    off_b = pl.multiple_of(off_a + SEL_CHUNK, SEL_CHUNK)
    ws = pl.multiple_of(jnp.maximum(t0 - WINDOW, 0), Q_BLOCK)
    max_a = sel_scores(off_a, sa_ref, True)
    max_b = sel_scores(off_b, sb_ref, True)
    m_s, l_s = sel_values(off_a, sa_ref, max_a, m_s, l_s)
    max_w0 = win_scores(ws, sa_ref, SEL_CHUNK)
    m_s, l_s = sel_values(off_b, sb_ref, max_b, m_s, l_s)
    inv_s = 1.0 / l_s

    acc_ref[2] = jnp.zeros((HEAD_DIM, lanes_all), F32)
    last_n = WIN_SPAN - 2 * SEL_CHUNK
    ws1 = pl.multiple_of(ws + SEL_CHUNK, Q_BLOCK)
    ws2 = pl.multiple_of(ws + 2 * SEL_CHUNK, Q_BLOCK)
    max_w1 = win_scores(ws1, sb_ref, SEL_CHUNK)
    m_w, l_w = value_tile(vwt_ref[0, 0, :, pl.ds(ws, SEL_CHUNK)], sa_ref, SEL_CHUNK, max_w0, neg_row, zero_row, 2)
    max_w2 = win_scores(ws2, sa_ref, last_n)
    m_w, l_w = value_tile(vwt_ref[0, 0, :, pl.ds(ws1, SEL_CHUNK)], sb_ref, SEL_CHUNK, max_w1, m_w, l_w, 2)
    m_w, l_w = value_tile(vwt_ref[0, 0, :, pl.ds(ws2, last_n)], sa_ref, last_n, max_w2, m_w, l_w, 2)
    inv_w = 1.0 / l_w

    gt_ref[...] = _sigmoid(gd_ref[...]).T
    for r in range(HEADS_PER_GROUP):
        sl = slice(Q_BLOCK * r, Q_BLOCK * (r + 1))
        base = 3 * (HEADS_PER_GROUP * g + r)
        o_t = (gt_ref[pl.ds(base, 1), :] * acc_ref[0, :, sl]
               + gt_ref[pl.ds(base + 1, 1), :] * (acc_ref[1, :, sl] * inv_s[:, sl])
               + gt_ref[pl.ds(base + 2, 1), :] * (acc_ref[2, :, sl] * inv_w[:, sl]))
        o_ref[0, :, HEAD_DIM * r:HEAD_DIM * (r + 1)] = o_t.T


def _attention(q_r, k_cmp, v_cmp_t, k_slc, v_slc_t, k_win, v_win_t, proj, b, s):
    g = N_KV_GROUPS
    nqb = s // Q_BLOCK
    n_cmp = s // CMP_STRIDE
    n_blk = s // SLC_BLOCK
    n_bias = pl.cdiv(n_blk, BLOCK_LANES)
    once = pl.Buffered(1)
    kspec = lambda n, w: pl.BlockSpec((1, 1, n, w), lambda i, j, k: (i, j, 0, 0), pipeline_mode=once)
    vspec = lambda n: pl.BlockSpec((1, 1, HEAD_DIM, n), lambda i, j, k: (i, j, 0, 0), pipeline_mode=once)
    return pl.pallas_call(
        _attn_kernel,
        grid=(b, g, nqb),
        in_specs=[pl.BlockSpec((1, HEADS_PER_GROUP, HEAD_DIM, Q_BLOCK), lambda i, j, k: (i, j, 0, k)),
                  kspec(n_cmp, HEAD_DIM), vspec(n_cmp), kspec(s, HEAD_DIM + BLOCK_LANES), vspec(s),
                  kspec(s, HEAD_DIM), vspec(s),
                  pl.BlockSpec((Q_BLOCK, 128), lambda i, j, k: (i * nqb + k, GD_BLOCK))],
        out_specs=pl.BlockSpec((1, Q_BLOCK, GROUP_WIDTH), lambda i, j, k: (i, k, j)),
        out_shape=jax.ShapeDtypeStruct((b, s, D_MODEL), F32),
        scratch_shapes=[pltpu.VMEM((n_cmp, GROUP_WIDTH), F32),
                        pltpu.VMEM((n_cmp, Q_BLOCK), F32),
                        pltpu.VMEM((n_bias * BLOCK_LANES, Q_BLOCK), F32),
                        pltpu.VMEM((3, HEAD_DIM, GROUP_WIDTH), F32),
                        pltpu.VMEM((128, Q_BLOCK), F32),
                        pltpu.VMEM((SEL_CHUNK, GROUP_WIDTH), F32),
                        pltpu.VMEM((SEL_CHUNK, GROUP_WIDTH), F32),
                        pltpu.VMEM((n_bias, HEAD_DIM + BLOCK_LANES, GROUP_WIDTH), BF16)],
        compiler_params=_params(("parallel", "parallel", "arbitrary")),
        name="nsa_attention",
    )(q_r, k_cmp, v_cmp_t, k_slc, v_slc_t, k_win, v_win_t, proj)


def _ssm_kernel(xbc_ref, gd_ref, z_ref, cw_ref, cb_ref, dtb_ref, alog_ref, dsk_ref, nw_ref, tri_ref,
                y_ref, tail_ref, st_ref, u_ref):
    L = SSD_CHUNK

    @pl.when(pl.program_id(1) == 0)
    def _():
        tail_ref[...] = jnp.zeros_like(tail_ref)
        st_ref[...] = jnp.zeros_like(st_ref)

    strip = 512
    for cs in range(CONV_DIM // strip):
        sl = slice(strip * cs, strip * (cs + 1))
        cur = xbc_ref[:, sl]
        ext = jnp.concatenate([tail_ref[:, sl], cur], axis=0)
        acc = cb_ref[:, sl] + cw_ref[0:1, sl] * ext[5:5 + L]
        for k in range(1, CONV_WIDTH):
            acc = acc + cw_ref[k:k + 1, sl] * ext[5 + k:5 + k + L]
        tail_ref[:, sl] = cur[L - 8:L]
        u_ref[:, sl] = acc * _sigmoid(acc)

    x_dt = gd_ref[...] + dtb_ref[...]
    dt = jnp.maximum(x_dt, 0.0) + jnp.log1p(jnp.exp(-jnp.abs(x_dt)))
    da = dt * -jnp.exp(alog_ref[...])
    tri = tri_ref[...]
    cum = jnp.zeros((L, 128), F32)
    for part in _split3(da):
        cum = cum + _dot(tri, part)
    cum_t = cum.T
    row = lax.broadcasted_iota(jnp.int32, (L, L), 0)
    col = lax.broadcasted_iota(jnp.int32, (L, L), 1)
    causal = row >= col

    last = cum[L - 1:L, :]
    per_head = jnp.concatenate(
        [dt, dt * jnp.exp(last - cum), jnp.exp(cum), jnp.broadcast_to(jnp.exp(last), (8, 128))], axis=0)
    per_head_parts = _split3(per_head)
    gw = SSM_HEADS_PER_GROUP * SSM_HEAD_DIM
    src_lane = lax.broadcasted_iota(jnp.int32, (128, gw), 0)
    head_of_lane = lax.broadcasted_iota(jnp.int32, (128, gw), 1) // SSM_HEAD_DIM
    out_head = lax.broadcasted_iota(jnp.int32, (L, gw), 1) // SSM_HEAD_DIM

    for g in range(N_SSM_GROUPS):
        b_g = u_ref[:, D_SSM + D_STATE * g:D_SSM + D_STATE * (g + 1)]
        c_g = u_ref[:, D_SSM + N_SSM_GROUPS * D_STATE + D_STATE * g:
                    D_SSM + N_SSM_GROUPS * D_STATE + D_STATE * (g + 1)]
        b_b = b_g.astype(BF16)
        c_b = c_g.astype(BF16)
        cb = _dot_nt(c_b, b_b)
        b_t = b_g.T.astype(BF16)
        st = st_ref[g]
        sl = slice(gw * g, gw * (g + 1))
        x_g = u_ref[:, sl]

        spread = jnp.where(src_lane == DT_LANE0 + SSM_HEADS_PER_GROUP * g + head_of_lane, 1.0, 0.0).astype(BF16)
        cols = _dot(per_head_parts[0], spread)
        for part in per_head_parts[1:]:
            cols = cols + _dot(part, spread)
        xdt = x_g * cols[0:L]
        x_end = x_g * cols[L:2 * L]
        y_g = _dot(c_b, st.astype(BF16)) * cols[2 * L:3 * L] + dsk_ref[:, sl] * x_g
        for r in range(SSM_HEADS_PER_GROUP):
            lane = DT_LANE0 + SSM_HEADS_PER_GROUP * g + r
            cc = cum[:, lane:lane + 1]
            cr = cum_t[lane:lane + 1, :]
            decay = jnp.exp(jnp.where(causal, cc - cr, -jnp.inf))
            w = (cb * decay).astype(BF16)
            y_g = y_g + _dot(w, jnp.where(out_head == r, xdt, 0.0).astype(BF16))
        st_ref[g] = st * cols[3 * L:3 * L + 1] + _dot(b_t, x_end.astype(BF16))
        z_g = z_ref[:, sl]
        v = y_g * (z_g * _sigmoid(z_g))
        ms = jnp.mean(v * v, axis=-1, keepdims=True)
        y_ref[:, sl] = (v * lax.rsqrt(ms + EPS) * nw_ref[:, sl]).astype(BF16)


def _ssm(proj, conv_w, conv_b, dtb, alog, dsk, norm_w, tri, b, s):
    nc = s // SSD_CHUNK
    full = lambda shape: pl.BlockSpec(shape, lambda i, j: (0,) * len(shape))
    return pl.pallas_call(
        _ssm_kernel,
        grid=(b, nc),
        in_specs=[pl.BlockSpec((SSD_CHUNK, CONV_DIM), lambda i, j: (i * nc + j, 1)),
                  pl.BlockSpec((SSD_CHUNK, 128), lambda i, j: (i * nc + j, GD_BLOCK)),
                  pl.BlockSpec((SSD_CHUNK, D_SSM), lambda i, j: (i * nc + j, 5)),
                  full((CONV_WIDTH, CONV_DIM)), full((1, CONV_DIM)),
                  full((1, 128)), full((1, 128)), full((1, D_SSM)),
                  full((1, D_SSM)), full((SSD_CHUNK, SSD_CHUNK))],
        out_specs=pl.BlockSpec((SSD_CHUNK, D_SSM), lambda i, j: (i * nc + j, 0)),
        out_shape=jax.ShapeDtypeStruct((b * s, D_SSM), BF16),
        scratch_shapes=[pltpu.VMEM((8, CONV_DIM), F32),
                        pltpu.VMEM((N_SSM_GROUPS, D_STATE, SSM_HEADS_PER_GROUP * SSM_HEAD_DIM), F32),
                        pltpu.VMEM((SSD_CHUNK, CONV_DIM), F32)],
        compiler_params=_params(("parallel", "arbitrary")),
        name="ssd_scan",
    )(proj, proj, proj, conv_w, conv_b, dtb, alog, dsk, norm_w, tri)


def _out_kernel(o_ref, z_ref, ys_ref, x_ref, naw_ref, w_ref, fw_ref, out_ref, *, final):
    z = z_ref[...]
    v = o_ref[...] * (z * _sigmoid(z))
    ms = jnp.mean(v * v, axis=-1, keepdims=True)
    ya = (v * lax.rsqrt(ms + EPS) * naw_ref[...]).astype(BF16)
    acc = x_ref[...] + _dot(ya, w_ref[0:D_MODEL, :]) + _dot(ys_ref[...], w_ref[D_MODEL:2 * D_MODEL, :])
    if final:
        ms2 = jnp.mean(acc * acc, axis=-1, keepdims=True)
        acc = acc * lax.rsqrt(ms2 + EPS) * fw_ref[...]
    out_ref[...] = acc


def _out_proj(o_attn, proj, y_ssm, x2d, norm_attn_w, w_out, final_w, final):
    t = x2d.shape[0]
    rowblk = lambda c: pl.BlockSpec((OUT_TM, D_MODEL), lambda i: (i, c))
    vec = pl.BlockSpec((1, D_MODEL), lambda i: (0, 0))
    return pl.pallas_call(
        functools.partial(_out_kernel, final=final),
        grid=(t // OUT_TM,),
        in_specs=[rowblk(0), rowblk(4), rowblk(0), rowblk(0), vec,
                  pl.BlockSpec((2 * D_MODEL, D_MODEL), lambda i: (0, 0), pipeline_mode=pl.Buffered(1)),
                  vec],
        out_specs=rowblk(0),
        out_shape=jax.ShapeDtypeStruct((t, D_MODEL), F32),
        compiler_params=_params(("parallel",)),
        name="out_proj",
    )(o_attn, proj, y_ssm, x2d, norm_attn_w, w_out, final_w)


def _pack_w_in(w):
    q0, kv0, gate0, za0, xbc0, dt0, zs0, end = 0, 2048, 3584, 3632, 5680, 9776, 9808, 11856
    pad = jnp.zeros((w.shape[0], 2048 - KV_COLS - GATE_LANES - N_HEADS_SSM), w.dtype)
    return jnp.concatenate(
        [w[:, kv0:gate0], w[:, gate0:za0], w[:, dt0:zs0], pad,
         w[:, q0:kv0], w[:, xbc0:dt0], w[:, za0:xbc0], w[:, zs0:end]], axis=1).astype(BF16)


def _dt_lanes(v):
    return jnp.zeros((1, 128), F32).at[0, DT_LANE0:DT_LANE0 + N_HEADS_SSM].set(v.astype(F32))


def kernel(x, positions, norm_w, w_in, cmp_pe_k, cmp_w1_k, cmp_w2_k, cmp_pe_v, cmp_w1_v, cmp_w2_v,
           conv_w, conv_b, dt_bias, a_log, d_skip, norm_attn_w, norm_ssm_w, w_out, final_norm_w):
    b, s, d = x.shape
    depth = w_in.shape[0]
    assert d == D_MODEL and s % 2048 == 0
    n_chunks = s // CMP_STRIDE

    half = HEAD_DIM // 2
    inv_freq = ROPE_THETA ** (-jnp.arange(half, dtype=F32) / half)
    inv_full = jnp.concatenate([inv_freq, inv_freq])[None, :]
    pos_f = positions.astype(F32)
    cos_t, sin_t = _rope_tables(pos_f[:, :, None], inv_full)
    pos_c = jnp.concatenate([pos_f[:, CMP_BLOCK - 1::CMP_STRIDE], pos_f[:, -1:]], axis=1)
    cos_c, sin_c = _rope_tables(pos_c[:, :, None], inv_full)

    tri = jnp.tril(jnp.ones((SSD_CHUNK, SSD_CHUNK), F32)).astype(BF16)
    x2d = x.reshape(b * s, d)
    for l in range(depth):
        w_pack = _pack_w_in(w_in[l])
        w1 = jnp.stack([cmp_w1_k[l], cmp_w1_v[l]]).reshape(2, 2, CMP_STRIDE * HEAD_DIM, CMP_HIDDEN).astype(BF16)
        w2 = jnp.stack([cmp_w2_k[l], cmp_w2_v[l]]).astype(BF16)
        pe = jnp.stack([cmp_pe_k[l], cmp_pe_v[l]]).reshape(2, 2, 1, CMP_STRIDE * HEAD_DIM)

        proj = _in_proj(x2d, norm_w[l][None, :], w_pack)
        q_r, kcvc, k_slc, k_win, v_slc_t, v_win_t = _prep(proj, cos_t, sin_t, b, s)
        kcvc = kcvc.reshape(b, 2, N_KV_GROUPS, n_chunks, CMP_STRIDE * HEAD_DIM)
        k_cmp, v_cmp_t = _compress(kcvc, w1, w2, pe, cos_c, sin_c)
        o_attn = _attention(q_r, k_cmp, v_cmp_t, k_slc, v_slc_t, k_win, v_win_t, proj, b, s)
        y_ssm = _ssm(proj, conv_w[l], conv_b[l][None, :], _dt_lanes(dt_bias[l]),
                     _dt_lanes(a_log[l]), jnp.repeat(d_skip[l].astype(F32), SSM_HEAD_DIM)[None, :],
                     norm_ssm_w[l][None, :], tri, b, s)
        x2d = _out_proj(o_attn.reshape(b * s, d), proj, y_ssm, x2d, norm_attn_w[l][None, :],
                        w_out[l].astype(BF16), final_norm_w[None, :], final=(l == depth - 1))
    return x2d.reshape(b, s, d)
```

```python
import functools

import jax
import jax.numpy as jnp
from jax import lax
from jax.experimental import pallas as pl
from jax.experimental.pallas import tpu as pltpu

F32 = jnp.float32
BF16 = jnp.bfloat16

D_MODEL = 2048
HEAD_DIM = 128
N_HEADS_ATTN = 16
N_KV_GROUPS = 2
HEADS_PER_GROUP = 8
GROUP_WIDTH = HEADS_PER_GROUP * HEAD_DIM
CMP_STRIDE = 16
CMP_BLOCK = 32
CMP_HIDDEN = 256
SLC_BLOCK = 64
N_SELECT = 16
WINDOW = 512
Q_BLOCK = 128
ROPE_THETA = 10000.0
NEG_INF = -1e30
SEL_FORCED = 1e4
N_HEADS_SSM = 32
SSM_HEAD_DIM = 64
N_SSM_GROUPS = 8
SSM_HEADS_PER_GROUP = 4
D_STATE = 128
D_SSM = 2048
CONV_WIDTH = 4
CONV_DIM = 4096
SSD_CHUNK = 256
EPS = 1e-6
Q_SCALE = HEAD_DIM ** -0.5 * 1.4426950408889634

KV_COLS = 6 * N_KV_GROUPS * HEAD_DIM
GD_BLOCK = KV_COLS // 128
GATE_LANES = 3 * N_HEADS_ATTN
DT_LANE0 = GATE_LANES
D_PACK = 12288
PROJ_TM = 512
PROJ_TN = 2048
PREP_TQ = 256
OUT_TM = 256
CMP_CHUNK = 256
SEL_CHUNK = 256
BLOCK_LANES = 128
BLOCK_SPAN = BLOCK_LANES * SLC_BLOCK
WIN_SPAN = WINDOW + Q_BLOCK
VMEM_LIMIT = 56 * 1024 * 1024


def _dot(a, b):
    return jnp.dot(a, b, preferred_element_type=F32)


def _dot_nt(a, b):
    return lax.dot_general(a, b, (((1,), (1,)), ((), ())), preferred_element_type=F32)


def _split3(x):
    hi = x.astype(BF16)
    r1 = x - hi.astype(F32)
    mid = r1.astype(BF16)
    lo = (r1 - mid.astype(F32)).astype(BF16)
    return hi, mid, lo


def _sigmoid(x):
    return 1.0 / (1.0 + jnp.exp(-x))


def _params(sem):
    return pltpu.CompilerParams(dimension_semantics=sem, vmem_limit_bytes=VMEM_LIMIT)


def _rope_table_kernel(pos_ref, inv_ref, cos_ref, sin_ref):
    ang = pos_ref[0] * inv_ref[...]
    lane = lax.broadcasted_iota(jnp.int32, ang.shape, 1)
    cos_ref[0] = jnp.cos(ang)
    s = jnp.sin(ang)
    sin_ref[0] = jnp.where(lane < HEAD_DIM // 2, -s, s)


def _rope_tables(pos_f, inv_full):
    b, n, _ = pos_f.shape
    rows = n if n <= 2048 else 2048
    out = jax.ShapeDtypeStruct((b, n, HEAD_DIM), F32)
    return pl.pallas_call(
        _rope_table_kernel,
        grid=(b, n // rows),
        in_specs=[pl.BlockSpec((1, rows, 1), lambda i, j: (i, j, 0)),
                  pl.BlockSpec((1, HEAD_DIM), lambda i, j: (0, 0))],
        out_specs=[pl.BlockSpec((1, rows, HEAD_DIM), lambda i, j: (i, j, 0))] * 2,
        out_shape=[out, out],
        compiler_params=_params(("parallel", "parallel")),
        name="rope_tables",
    )(pos_f, inv_full)


def _proj_kernel(x_ref, nw_ref, w_ref, o_ref, h_ref):
    @pl.when(pl.program_id(1) == 0)
    def _():
        x = x_ref[...]
        ms = jnp.mean(x * x, axis=-1, keepdims=True)
        h_ref[...] = (x * lax.rsqrt(ms + EPS) * nw_ref[...]).astype(BF16)

    o_ref[...] = _dot(h_ref[...], w_ref[...])


def _in_proj(x2d, norm_w, w_pack):
    t = x2d.shape[0]
    return pl.pallas_call(
        _proj_kernel,
        grid=(t // PROJ_TM, D_PACK // PROJ_TN),
        in_specs=[pl.BlockSpec((PROJ_TM, D_MODEL), lambda i, j: (i, 0)),
                  pl.BlockSpec((1, D_MODEL), lambda i, j: (0, 0)),
                  pl.BlockSpec((D_MODEL, PROJ_TN), lambda i, j: (0, j))],
        out_specs=pl.BlockSpec((PROJ_TM, PROJ_TN), lambda i, j: (i, j)),
        out_shape=jax.ShapeDtypeStruct((t, D_PACK), F32),
        scratch_shapes=[pltpu.VMEM((PROJ_TM, D_MODEL), BF16)],
        compiler_params=_params(("parallel", "arbitrary")),
        name="in_proj",
    )(x2d, norm_w, w_pack)


def _prep_kernel(kv_ref, q_ref, cos_ref, sin_ref,
                 qo_ref, kcvc_ref, ks_ref, kw_ref, vst_ref, vwt_ref):
    c = cos_ref[0]
    s = sin_ref[0]

    def rope(u):
        return u * c + pltpu.roll(u, HEAD_DIM // 2, 1) * s

    for h in range(N_HEADS_ATTN):
        sl = slice(HEAD_DIM * h, HEAD_DIM * (h + 1))
        qo_ref[0, h] = (rope(q_ref[:, sl]) * Q_SCALE).T.astype(BF16)

    def seg(i, g):
        lo = (i * N_KV_GROUPS + g) * HEAD_DIM
        return kv_ref[:, lo:lo + HEAD_DIM]

    tq = q_ref.shape[0]
    tok = pl.program_id(1) * tq + lax.broadcasted_iota(jnp.int32, (tq, BLOCK_LANES), 0)
    lane = lax.broadcasted_iota(jnp.int32, (tq, BLOCK_LANES), 1)
    block_onehot = jnp.where((tok // SLC_BLOCK) % BLOCK_LANES == lane, 1.0, 0.0).astype(BF16)

    for g in range(N_KV_GROUPS):
        kcvc_ref[0, 0, g] = seg(0, g).astype(BF16)
        kcvc_ref[0, 1, g] = seg(1, g).astype(BF16)
        ks_ref[0, g, :, 0:HEAD_DIM] = rope(seg(2, g)).astype(BF16)
        ks_ref[0, g, :, HEAD_DIM:HEAD_DIM + BLOCK_LANES] = block_onehot
        vst_ref[0, g] = seg(3, g).T.astype(BF16)
        kw_ref[0, g] = rope(seg(4, g)).astype(BF16)
        vwt_ref[0, g] = seg(5, g).T.astype(BF16)


def _prep(proj, cos_t, sin_t, b, s):
    nq = s // PREP_TQ
    g = N_KV_GROUPS
    row = lambda i, j: (i * nq + j, 0)
    kshape = jax.ShapeDtypeStruct((b, g, s, HEAD_DIM), BF16)
    vshape = jax.ShapeDtypeStruct((b, g, HEAD_DIM, s), BF16)
    kspec = pl.BlockSpec((1, g, PREP_TQ, HEAD_DIM), lambda i, j: (i, 0, j, 0))
    vspec = pl.BlockSpec((1, g, HEAD_DIM, PREP_TQ), lambda i, j: (i, 0, 0, j))
    ks_width = HEAD_DIM + BLOCK_LANES
    ks_shape = jax.ShapeDtypeStruct((b, g, s, ks_width), BF16)
    ks_spec = pl.BlockSpec((1, g, PREP_TQ, ks_width), lambda i, j: (i, 0, j, 0))
    return pl.pallas_call(
        _prep_kernel,
        grid=(b, nq),
        in_specs=[pl.BlockSpec((PREP_TQ, KV_COLS), row),
                  pl.BlockSpec((PREP_TQ, D_MODEL), lambda i, j: (i * nq + j, 1)),
                  pl.BlockSpec((1, PREP_TQ, HEAD_DIM), lambda i, j: (i, j, 0)),
                  pl.BlockSpec((1, PREP_TQ, HEAD_DIM), lambda i, j: (i, j, 0))],
        out_specs=[pl.BlockSpec((1, N_HEADS_ATTN, HEAD_DIM, PREP_TQ), lambda i, j: (i, 0, 0, j)),
                   pl.BlockSpec((1, 2, g, PREP_TQ, HEAD_DIM), lambda i, j: (i, 0, 0, j, 0)),
                   ks_spec, kspec, vspec, vspec],
        out_shape=[jax.ShapeDtypeStruct((b, N_HEADS_ATTN, HEAD_DIM, s), BF16),
                   jax.ShapeDtypeStruct((b, 2, g, s, HEAD_DIM), BF16),
                   ks_shape, kshape, vshape, vshape],
        compiler_params=_params(("parallel", "parallel")),
        name="rope_relayout",
    )(proj, proj, cos_t, sin_t)


def _cmp_kernel(x_ref, w1_ref, w2_ref, pe_ref, cos_ref, sin_ref, kc_ref, vct_ref):
    n = x_ref.shape[3]
    for kv in range(2):
        x = x_ref[0, kv, 0]
        a0 = _dot(x, w1_ref[kv, 0])
        a1 = _dot(x, w1_ref[kv, 1])
        bias = jnp.zeros((8, CMP_HIDDEN), F32)
        for r in range(2):
            pe = jnp.broadcast_to(pe_ref[kv, r], (8, CMP_STRIDE * HEAD_DIM))
            for part in _split3(pe):
                bias = bias + _dot(part, w1_ref[kv, r])
        hid = a0 + pltpu.roll(a1, n - 1, 0) + bias[0:1]
        act = hid * _sigmoid(hid)
        out = _dot(act.astype(BF16), w2_ref[kv])
        if kv == 0:
            roped = out * cos_ref[0] + pltpu.roll(out, HEAD_DIM // 2, 1) * sin_ref[0]
            kc_ref[0, 0] = roped.astype(BF16)
        else:
            vct_ref[0, 0] = out.T.astype(BF16)


def _compress(kcvc, w1, w2, pe, cos_c, sin_c):
    b, _, g, n, width = kcvc.shape
    return pl.pallas_call(
        _cmp_kernel,
        grid=(b, g),
        in_specs=[pl.BlockSpec((1, 2, 1, n, width), lambda i, j: (i, 0, j, 0, 0)),
                  pl.BlockSpec(w1.shape, lambda i, j: (0, 0, 0, 0)),
                  pl.BlockSpec(w2.shape, lambda i, j: (0, 0, 0)),
                  pl.BlockSpec(pe.shape, lambda i, j: (0, 0, 0, 0)),
                  pl.BlockSpec((1, n, HEAD_DIM), lambda i, j: (i, 0, 0)),
                  pl.BlockSpec((1, n, HEAD_DIM), lambda i, j: (i, 0, 0))],
        out_specs=[pl.BlockSpec((1, 1, n, HEAD_DIM), lambda i, j: (i, j, 0, 0)),
                   pl.BlockSpec((1, 1, HEAD_DIM, n), lambda i, j: (i, j, 0, 0))],
        out_shape=[jax.ShapeDtypeStruct((b, g, n, HEAD_DIM), BF16),
                   jax.ShapeDtypeStruct((b, g, HEAD_DIM, n), BF16)],
        compiler_params=_params(("parallel", "parallel")),
        name="compress",
    )(kcvc, w1, w2, pe, cos_c, sin_c)


def _attn_kernel(q_ref, kc_ref, vct_ref, ks_ref, vst_ref, kw_ref, vwt_ref, gd_ref, o_ref,
                 sc_ref, imp_ref, sel_ref, acc_ref, gt_ref, sa_ref, sb_ref, qp_ref):
    n_cmp = kc_ref.shape[2]
    n_blk = imp_ref.shape[0] // (SLC_BLOCK // CMP_STRIDE)
    g = pl.program_id(1)
    qi = pl.program_id(2)
    t0 = qi * Q_BLOCK
    lanes_all = GROUP_WIDTH

    qt = jnp.concatenate([q_ref[0, r] for r in range(HEADS_PER_GROUP)], axis=1)
    t_lane = t0 + lax.broadcasted_iota(jnp.int32, (1, Q_BLOCK), 1)

    def tile_heads(x):
        return jnp.concatenate([x] * HEADS_PER_GROUP, axis=1)

    neg_row = jnp.full((1, lanes_all), NEG_INF, F32)
    zero_row = jnp.zeros((1, lanes_all), F32)

    chunk = min(CMP_CHUNK, n_cmp)
    n_c = (8 * qi + 6) // chunk + 1
    imp_ref[...] = jnp.zeros_like(imp_ref)
    acc_ref[0] = jnp.zeros((HEAD_DIM, lanes_all), F32)

    def cmp_scores(c, carry):
        m, l = carry
        off = pl.multiple_of(c * chunk, chunk)
        ci = off + lax.broadcasted_iota(jnp.int32, (chunk, Q_BLOCK), 0)
        vis = ci * CMP_STRIDE + (CMP_BLOCK - 1) <= t_lane
        s = _dot(kc_ref[0, 0, pl.ds(off, chunk), :], qt) + tile_heads(jnp.where(vis, 0.0, NEG_INF))
        sc_ref[pl.ds(off, chunk), :] = s
        m_new = jnp.maximum(m, jnp.max(s, axis=0, keepdims=True))
        l = l * jnp.exp2(m - m_new) + jnp.sum(jnp.exp2(s - m_new), axis=0, keepdims=True)
        return m_new, l

    def by_pairs(body, init):
        cr = lax.fori_loop(0, n_c // 2, lambda i, cr: body(2 * i + 1, body(2 * i, cr)), init)
        return lax.fori_loop(n_c - n_c % 2, n_c, body, cr)

    m_c, l_c = by_pairs(cmp_scores, (neg_row, zero_row))
    w_c = jnp.where(m_c > 0.5 * NEG_INF, 1.0 / l_c, 0.0)

    def cmp_out(c, carry):
        off = pl.multiple_of(c * chunk, chunk)
        pn = jnp.exp2(sc_ref[pl.ds(off, chunk), :] - m_c) * w_c
        imp = pn[:, 0:Q_BLOCK]
        for r in range(1, HEADS_PER_GROUP):
            imp = imp + pn[:, Q_BLOCK * r:Q_BLOCK * (r + 1)]
        imp_ref[pl.ds(off, chunk), :] = imp
        acc_ref[0] += _dot(vct_ref[0, 0, :, pl.ds(off, chunk)], pn.astype(BF16))
        return carry

    by_pairs(cmp_out, 0)

    ratio = SLC_BLOCK // CMP_STRIDE
    score_in = pltpu.roll(imp_ref[pl.ds(ratio - 1, n_blk, stride=ratio), :], 1, 0)
    for m in range(ratio):
        score_in = score_in + imp_ref[pl.ds(m, n_blk, stride=ratio), :]
    blk = lax.broadcasted_iota(jnp.int32, (n_blk, Q_BLOCK), 0)
    cur = t_lane // SLC_BLOCK
    forced = (blk == 0) | (blk == cur) | (blk == cur - 1)
    visible = blk <= cur
    score0 = jnp.where(forced, -jnp.inf, jnp.where(visible, score_in, -SEL_FORCED))
    blk_f = blk.astype(F32)

    def search(score, sel, ids):
        def pick(_, carry):
            score, sel = carry
            mx = jnp.max(score, axis=0, keepdims=True)
            first = jnp.min(jnp.where(score == mx, ids, float(n_blk)), axis=0, keepdims=True)
            hit = ids == first
            return jnp.where(hit, -jnp.inf, score), jnp.where(hit, 1.0, sel)

        return lax.fori_loop(0, min(N_SELECT, n_blk) - 3, pick, (score, sel))[1]

    sel0 = jnp.where(forced, 1.0, 0.0)
    if n_blk > BLOCK_LANES:
        low = BLOCK_LANES
        sel = lax.cond(
            t0 + Q_BLOCK <= low * SLC_BLOCK,
            lambda: jnp.concatenate([search(score0[:low], sel0[:low], blk_f[:low]), sel0[low:]], axis=0),
            lambda: search(score0, sel0, blk_f))
    else:
        sel = search(score0, sel0, blk_f)
    sel_ref[...] = jnp.zeros_like(sel_ref)
    sel_ref[0:n_blk, :] = jnp.where(visible, sel, 0.0)

    for h in range(qp_ref.shape[0]):
        picked = sel_ref[BLOCK_LANES * h:BLOCK_LANES * (h + 1), :]
        qp_ref[h, 0:HEAD_DIM, :] = qt
        qp_ref[h, HEAD_DIM:HEAD_DIM + BLOCK_LANES, :] = tile_heads(
            jnp.where(picked > 0.5, 0.0, NEG_INF).astype(BF16))

    def finish_scores(s, s_ref, n):
        s_ref[0:n, :] = s
        return jnp.max(s, axis=0, keepdims=True)

    def causal_bias(off, n):
        kpos = off + lax.broadcasted_iota(jnp.int32, (n, Q_BLOCK), 0)
        return tile_heads(jnp.where(kpos <= t_lane, 0.0, NEG_INF))

    def sel_scores(off, s_ref, causal):
        s = _dot(ks_ref[0, 0, pl.ds(off, SEL_CHUNK), :], qp_ref[off // BLOCK_SPAN])
        if causal:
            s = s + causal_bias(off, SEL_CHUNK)
        return finish_scores(s, s_ref, SEL_CHUNK)

    def win_scores(off, s_ref, n):
        kpos = off + lax.broadcasted_iota(jnp.int32, (n, Q_BLOCK), 0)
        in_win = (kpos <= t_lane) & (kpos > t_lane - WINDOW)
        s = _dot(kw_ref[0, 0, pl.ds(off, n), :], qt) + tile_heads(jnp.where(in_win, 0.0, NEG_INF))
        return finish_scores(s, s_ref, n)

    def value_tile(vt, s_ref, n, tile_max, m, l, branch):
        m_new = jnp.maximum(m, tile_max)
        alpha = jnp.exp2(m - m_new)
        p = jnp.exp2(s_ref[0:n, :] - m_new)
        acc_ref[branch] = alpha * acc_ref[branch] + _dot(vt, p.astype(BF16))
        return m_new, alpha * l + jnp.sum(p, axis=0, keepdims=True)

    def sel_values(off, s_ref, tile_max, m, l):
        return value_tile(vst_ref[0, 0, :, pl.ds(off, SEL_CHUNK)], s_ref, SEL_CHUNK, tile_max, m, l, 1)

    trip = 2 * SEL_CHUNK
    n_k = (t0 + Q_BLOCK - 1) // trip + 1
    acc_ref[1] = jnp.zeros((HEAD_DIM, lanes_all), F32)

    def sel_step(c, carry):
        m, l, max_a = carry
        off_a = pl.multiple_of(c * trip, trip)
        off_b = pl.multiple_of(off_a + SEL_CHUNK, SEL_CHUNK)
        max_b = sel_scores(off_b, sb_ref, False)
        m, l = sel_values(off_a, sa_ref, max_a, m, l)
        max_a = sel_scores(pl.multiple_of(off_a + trip, trip), sa_ref, False)
        m, l = sel_values(off_b, sb_ref, max_b, m, l)
        return m, l, max_a

    max_a0 = sel_scores(0, sa_ref, False)
    n_main = n_k - 1
    def trips(first, count, cr):
        for j in range(count):
            cr = sel_step(first + j, cr)
        return cr

    n_quad = n_main // 4
    carry = lax.fori_loop(0, n_quad, lambda i, cr: trips(4 * i, 4, cr), (neg_row, zero_row, max_a0))
    n_pair = (n_main - 4 * n_quad) // 2
    carry = lax.fori_loop(0, n_pair, lambda i, cr: trips(4 * n_quad, 2, cr), carry)
    m_s, l_s, _ = lax.fori_loop(4 * n_quad + 2 * n_pair, n_main, sel_step, carry)

    off_a = pl.multiple_of((n_k - 1) * trip, trip)
    off_b = pl.multiple_of(off_a + SEL_CHUNK, SEL_CHUNK)
    ws = pl.multiple_of(jnp.maximum(t0 - WINDOW, 0), Q_BLOCK)
    max_a = finish_scores(sa_ref[...] + causal_bias(off_a, SEL_CHUNK), sa_ref, SEL_CHUNK)
    max_b = sel_scores(off_b, sb_ref, True)
    m_s, l_s = sel_values(off_a, sa_ref, max_a, m_s, l_s)
    max_w0 = win_scores(ws, sa_ref, SEL_CHUNK)
    m_s, l_s = sel_values(off_b, sb_ref, max_b, m_s, l_s)
    inv_s = 1.0 / l_s

    acc_ref[2] = jnp.zeros((HEAD_DIM, lanes_all), F32)
    last_n = WIN_SPAN - 2 * SEL_CHUNK
    ws1 = pl.multiple_of(ws + SEL_CHUNK, Q_BLOCK)
    ws2 = pl.multiple_of(ws + 2 * SEL_CHUNK, Q_BLOCK)
    max_w1 = win_scores(ws1, sb_ref, SEL_CHUNK)
    m_w, l_w = value_tile(vwt_ref[0, 0, :, pl.ds(ws, SEL_CHUNK)], sa_ref, SEL_CHUNK, max_w0, neg_row, zero_row, 2)
    max_w2 = win_scores(ws2, sa_ref, last_n)
    m_w, l_w = value_tile(vwt_ref[0, 0, :, pl.ds(ws1, SEL_CHUNK)], sb_ref, SEL_CHUNK, max_w1, m_w, l_w, 2)
    m_w, l_w = value_tile(vwt_ref[0, 0, :, pl.ds(ws2, last_n)], sa_ref, last_n, max_w2, m_w, l_w, 2)
    inv_w = 1.0 / l_w

    gt_ref[...] = _sigmoid(gd_ref[...]).T
    for r in range(HEADS_PER_GROUP):
        sl = slice(Q_BLOCK * r, Q_BLOCK * (r + 1))
        base = 3 * (HEADS_PER_GROUP * g + r)
        o_t = (gt_ref[pl.ds(base, 1), :] * acc_ref[0, :, sl]
               + gt_ref[pl.ds(base + 1, 1), :] * (acc_ref[1, :, sl] * inv_s[:, sl])
               + gt_ref[pl.ds(base + 2, 1), :] * (acc_ref[2, :, sl] * inv_w[:, sl]))
        o_ref[0, :, HEAD_DIM * r:HEAD_DIM * (r + 1)] = o_t.T


def _attention(q_r, k_cmp, v_cmp_t, k_slc, v_slc_t, k_win, v_win_t, proj, b, s):
    g = N_KV_GROUPS
    nqb = s // Q_BLOCK
    n_cmp = s // CMP_STRIDE
    n_blk = s // SLC_BLOCK
    n_bias = pl.cdiv(n_blk, BLOCK_LANES)
    once = pl.Buffered(1)
    kspec = lambda n, w: pl.BlockSpec((1, 1, n, w), lambda i, j, k: (i, j, 0, 0), pipeline_mode=once)
    vspec = lambda n: pl.BlockSpec((1, 1, HEAD_DIM, n), lambda i, j, k: (i, j, 0, 0), pipeline_mode=once)
    return pl.pallas_call(
        _attn_kernel,
        grid=(b, g, nqb),
        in_specs=[pl.BlockSpec((1, HEADS_PER_GROUP, HEAD_DIM, Q_BLOCK), lambda i, j, k: (i, j, 0, k)),
                  kspec(n_cmp, HEAD_DIM), vspec(n_cmp), kspec(s, HEAD_DIM + BLOCK_LANES), vspec(s),
                  kspec(s, HEAD_DIM), vspec(s),
                  pl.BlockSpec((Q_BLOCK, 128), lambda i, j, k: (i * nqb + k, GD_BLOCK))],
        out_specs=pl.BlockSpec((1, Q_BLOCK, GROUP_WIDTH), lambda i, j, k: (i, k, j)),
        out_shape=jax.ShapeDtypeStruct((b, s, D_MODEL), F32),
        scratch_shapes=[pltpu.VMEM((n_cmp, GROUP_WIDTH), F32),
                        pltpu.VMEM((n_cmp, Q_BLOCK), F32),
                        pltpu.VMEM((n_bias * BLOCK_LANES, Q_BLOCK), F32),
                        pltpu.VMEM((3, HEAD_DIM, GROUP_WIDTH), F32),
                        pltpu.VMEM((128, Q_BLOCK), F32),
                        pltpu.VMEM((SEL_CHUNK, GROUP_WIDTH), F32),
                        pltpu.VMEM((SEL_CHUNK, GROUP_WIDTH), F32),
                        pltpu.VMEM((n_bias, HEAD_DIM + BLOCK_LANES, GROUP_WIDTH), BF16)],
        compiler_params=_params(("parallel", "parallel", "arbitrary")),
        name="nsa_attention",
    )(q_r, k_cmp, v_cmp_t, k_slc, v_slc_t, k_win, v_win_t, proj)


def _ssm_kernel(xbc_ref, gd_ref, z_ref, cw_ref, cb_ref, dtb_ref, alog_ref, dsk_ref, nw_ref, tri_ref,
                y_ref, tail_ref, st_ref, u_ref):
    L = SSD_CHUNK

    @pl.when(pl.program_id(1) == 0)
    def _():
        tail_ref[...] = jnp.zeros_like(tail_ref)
        st_ref[...] = jnp.zeros_like(st_ref)

    strip = 512
    for cs in range(CONV_DIM // strip):
        sl = slice(strip * cs, strip * (cs + 1))
        cur = xbc_ref[:, sl]
        ext = jnp.concatenate([tail_ref[:, sl], cur], axis=0)
        acc = cb_ref[:, sl] + cw_ref[0:1, sl] * ext[5:5 + L]
        for k in range(1, CONV_WIDTH):
            acc = acc + cw_ref[k:k + 1, sl] * ext[5 + k:5 + k + L]
        tail_ref[:, sl] = cur[L - 8:L]
        u_ref[:, sl] = acc * _sigmoid(acc)

    x_dt = gd_ref[...] + dtb_ref[...]
    dt = jnp.maximum(x_dt, 0.0) + jnp.log1p(jnp.exp(-jnp.abs(x_dt)))
    da = dt * -jnp.exp(alog_ref[...])
    tri = tri_ref[...]
    cum = jnp.zeros((L, 128), F32)
    for part in _split3(da):
        cum = cum + _dot(tri, part)
    cum_t = cum.T
    row = lax.broadcasted_iota(jnp.int32, (L, L), 0)
    col = lax.broadcasted_iota(jnp.int32, (L, L), 1)
    causal = row >= col

    last = cum[L - 1:L, :]
    per_head = jnp.concatenate(
        [dt, dt * jnp.exp(last - cum), jnp.exp(cum), jnp.broadcast_to(jnp.exp(last), (8, 128))], axis=0)
    per_head_parts = _split3(per_head)
    gw = SSM_HEADS_PER_GROUP * SSM_HEAD_DIM
    src_lane = lax.broadcasted_iota(jnp.int32, (128, gw), 0)
    head_of_lane = lax.broadcasted_iota(jnp.int32, (128, gw), 1) // SSM_HEAD_DIM
    out_head = lax.broadcasted_iota(jnp.int32, (L, gw), 1) // SSM_HEAD_DIM

    for g in range(N_SSM_GROUPS):
        b_g = u_ref[:, D_SSM + D_STATE * g:D_SSM + D_STATE * (g + 1)]
        c_g = u_ref[:, D_SSM + N_SSM_GROUPS * D_STATE + D_STATE * g:
                    D_SSM + N_SSM_GROUPS * D_STATE + D_STATE * (g + 1)]
        b_b = b_g.astype(BF16)
        c_b = c_g.astype(BF16)
        cb = _dot_nt(c_b, b_b)
        b_t = b_g.T.astype(BF16)
        st = st_ref[g]
        sl = slice(gw * g, gw * (g + 1))
        x_g = u_ref[:, sl]

        spread = jnp.where(src_lane == DT_LANE0 + SSM_HEADS_PER_GROUP * g + head_of_lane, 1.0, 0.0).astype(BF16)
        cols = _dot(per_head_parts[0], spread)
        for part in per_head_parts[1:]:
            cols = cols + _dot(part, spread)
        xdt = x_g * cols[0:L]
        x_end = x_g * cols[L:2 * L]
        y_g = _dot(c_b, st.astype(BF16)) * cols[2 * L:3 * L] + dsk_ref[:, sl] * x_g
        for r in range(SSM_HEADS_PER_GROUP):
            lane = DT_LANE0 + SSM_HEADS_PER_GROUP * g + r
            cc = cum[:, lane:lane + 1]
            cr = cum_t[lane:lane + 1, :]
            decay = jnp.exp(jnp.where(causal, cc - cr, -jnp.inf))
            w = (cb * decay).astype(BF16)
            y_g = y_g + _dot(w, jnp.where(out_head == r, xdt, 0.0).astype(BF16))
        st_ref[g] = st * cols[3 * L:3 * L + 1] + _dot(b_t, x_end.astype(BF16))
        z_g = z_ref[:, sl]
        v = y_g * (z_g * _sigmoid(z_g))
        ms = jnp.mean(v * v, axis=-1, keepdims=True)
        y_ref[:, sl] = (v * lax.rsqrt(ms + EPS) * nw_ref[:, sl]).astype(BF16)


def _ssm(proj, conv_w, conv_b, dtb, alog, dsk, norm_w, tri, b, s):
    nc = s // SSD_CHUNK
    full = lambda shape: pl.BlockSpec(shape, lambda i, j: (0,) * len(shape))
    return pl.pallas_call(
        _ssm_kernel,
        grid=(b, nc),
        in_specs=[pl.BlockSpec((SSD_CHUNK, CONV_DIM), lambda i, j: (i * nc + j, 1)),
                  pl.BlockSpec((SSD_CHUNK, 128), lambda i, j: (i * nc + j, GD_BLOCK)),
                  pl.BlockSpec((SSD_CHUNK, D_SSM), lambda i, j: (i * nc + j, 5)),
                  full((CONV_WIDTH, CONV_DIM)), full((1, CONV_DIM)),
                  full((1, 128)), full((1, 128)), full((1, D_SSM)),
                  full((1, D_SSM)), full((SSD_CHUNK, SSD_CHUNK))],
        out_specs=pl.BlockSpec((SSD_CHUNK, D_SSM), lambda i, j: (i * nc + j, 0)),
        out_shape=jax.ShapeDtypeStruct((b * s, D_SSM), BF16),
        scratch_shapes=[pltpu.VMEM((8, CONV_DIM), F32),
                        pltpu.VMEM((N_SSM_GROUPS, D_STATE, SSM_HEADS_PER_GROUP * SSM_HEAD_DIM), F32),
                        pltpu.VMEM((SSD_CHUNK, CONV_DIM), F32)],
        compiler_params=_params(("parallel", "arbitrary")),
        name="ssd_scan",
    )(proj, proj, proj, conv_w, conv_b, dtb, alog, dsk, norm_w, tri)


def _out_kernel(o_ref, z_ref, ys_ref, x_ref, naw_ref, w_ref, fw_ref, out_ref, *, final):
    z = z_ref[...]
    v = o_ref[...] * (z * _sigmoid(z))
    ms = jnp.mean(v * v, axis=-1, keepdims=True)
    ya = (v * lax.rsqrt(ms + EPS) * naw_ref[...]).astype(BF16)
    acc = x_ref[...] + _dot(ya, w_ref[0:D_MODEL, :]) + _dot(ys_ref[...], w_ref[D_MODEL:2 * D_MODEL, :])
    if final:
        ms2 = jnp.mean(acc * acc, axis=-1, keepdims=True)
        acc = acc * lax.rsqrt(ms2 + EPS) * fw_ref[...]
    out_ref[...] = acc


def _out_proj(o_attn, proj, y_ssm, x2d, norm_attn_w, w_out, final_w, final):
    t = x2d.shape[0]
    rowblk = lambda c: pl.BlockSpec((OUT_TM, D_MODEL), lambda i: (i, c))
    vec = pl.BlockSpec((1, D_MODEL), lambda i: (0, 0))
    return pl.pallas_call(
        functools.partial(_out_kernel, final=final),
        grid=(t // OUT_TM,),
        in_specs=[rowblk(0), rowblk(4), rowblk(0), rowblk(0), vec,
                  pl.BlockSpec((2 * D_MODEL, D_MODEL), lambda i: (0, 0), pipeline_mode=pl.Buffered(1)),
                  vec],
        out_specs=rowblk(0),
        out_shape=jax.ShapeDtypeStruct((t, D_MODEL), F32),
        compiler_params=_params(("parallel",)),
        name="out_proj",
    )(o_attn, proj, y_ssm, x2d, norm_attn_w, w_out, final_w)


def _pack_w_in(w):
    q0, kv0, gate0, za0, xbc0, dt0, zs0, end = 0, 2048, 3584, 3632, 5680, 9776, 9808, 11856
    pad = jnp.zeros((w.shape[0], 2048 - KV_COLS - GATE_LANES - N_HEADS_SSM), w.dtype)
    return jnp.concatenate(
        [w[:, kv0:gate0], w[:, gate0:za0], w[:, dt0:zs0], pad,
         w[:, q0:kv0], w[:, xbc0:dt0], w[:, za0:xbc0], w[:, zs0:end]], axis=1).astype(BF16)


def _dt_lanes(v):
    return jnp.zeros((1, 128), F32).at[0, DT_LANE0:DT_LANE0 + N_HEADS_SSM].set(v.astype(F32))


def kernel(x, positions, norm_w, w_in, cmp_pe_k, cmp_w1_k, cmp_w2_k, cmp_pe_v, cmp_w1_v, cmp_w2_v,
           conv_w, conv_b, dt_bias, a_log, d_skip, norm_attn_w, norm_ssm_w, w_out, final_norm_w):
    b, s, d = x.shape
    depth = w_in.shape[0]
    assert d == D_MODEL and s % 2048 == 0
    n_chunks = s // CMP_STRIDE

    half = HEAD_DIM // 2
    inv_freq = ROPE_THETA ** (-jnp.arange(half, dtype=F32) / half)
    inv_full = jnp.concatenate([inv_freq, inv_freq])[None, :]
    pos_f = positions.astype(F32)
    cos_t, sin_t = _rope_tables(pos_f[:, :, None], inv_full)
    pos_c = jnp.concatenate([pos_f[:, CMP_BLOCK - 1::CMP_STRIDE], pos_f[:, -1:]], axis=1)
    cos_c, sin_c = _rope_tables(pos_c[:, :, None], inv_full)

    tri = jnp.tril(jnp.ones((SSD_CHUNK, SSD_CHUNK), F32)).astype(BF16)
    x2d = x.reshape(b * s, d)
    for l in range(depth):
        w_pack = _pack_w_in(w_in[l])
        w1 = jnp.stack([cmp_w1_k[l], cmp_w1_v[l]]).reshape(2, 2, CMP_STRIDE * HEAD_DIM, CMP_HIDDEN).astype(BF16)
        w2 = jnp.stack([cmp_w2_k[l], cmp_w2_v[l]]).astype(BF16)
        pe = jnp.stack([cmp_pe_k[l], cmp_pe_v[l]]).reshape(2, 2, 1, CMP_STRIDE * HEAD_DIM)

        proj = _in_proj(x2d, norm_w[l][None, :], w_pack)
        q_r, kcvc, k_slc, k_win, v_slc_t, v_win_t = _prep(proj, cos_t, sin_t, b, s)
        kcvc = kcvc.reshape(b, 2, N_KV_GROUPS, n_chunks, CMP_STRIDE * HEAD_DIM)
        k_cmp, v_cmp_t = _compress(kcvc, w1, w2, pe, cos_c, sin_c)
        o_attn = _attention(q_r, k_cmp, v_cmp_t, k_slc, v_slc_t, k_win, v_win_t, proj, b, s)
        y_ssm = _ssm(proj, conv_w[l], conv_b[l][None, :], _dt_lanes(dt_bias[l]),
                     _dt_lanes(a_log[l]), jnp.repeat(d_skip[l].astype(F32), SSM_HEAD_DIM)[None, :],
                     norm_ssm_w[l][None, :], tri, b, s)
        x2d = _out_proj(o_attn.reshape(b * s, d), proj, y_ssm, x2d, norm_attn_w[l][None, :],
                        w_out[l].astype(BF16), final_norm_w[None, :], final=(l == depth - 1))
    return x2d.reshape(b, s, d)
```

```python
import functools

import jax
import jax.numpy as jnp
from jax import lax
from jax.experimental import pallas as pl
from jax.experimental.pallas import tpu as pltpu

F32 = jnp.float32
BF16 = jnp.bfloat16

D_MODEL = 2048
HEAD_DIM = 128
N_HEADS_ATTN = 16
N_KV_GROUPS = 2
HEADS_PER_GROUP = 8
GROUP_WIDTH = HEADS_PER_GROUP * HEAD_DIM
CMP_STRIDE = 16
CMP_BLOCK = 32
CMP_HIDDEN = 256
SLC_BLOCK = 64
N_SELECT = 16
WINDOW = 512
Q_BLOCK = 128
ROPE_THETA = 10000.0
NEG_INF = -1e30
SEL_FORCED = 1e4
N_HEADS_SSM = 32
SSM_HEAD_DIM = 64
N_SSM_GROUPS = 8
SSM_HEADS_PER_GROUP = 4
D_STATE = 128
D_SSM = 2048
CONV_WIDTH = 4
CONV_DIM = 4096
SSD_CHUNK = 256
EPS = 1e-6
Q_SCALE = HEAD_DIM ** -0.5 * 1.4426950408889634

KV_COLS = 6 * N_KV_GROUPS * HEAD_DIM
GD_BLOCK = KV_COLS // 128
GATE_LANES = 3 * N_HEADS_ATTN
DT_LANE0 = GATE_LANES
D_PACK = 12288
PROJ_TM = 512
PROJ_TN = 2048
PREP_TQ = 256
OUT_TM = 256
CMP_CHUNK = 256
SEL_CHUNK = 256
BLOCK_LANES = 128
BLOCK_SPAN = BLOCK_LANES * SLC_BLOCK
WIN_SPAN = WINDOW + Q_BLOCK
VMEM_LIMIT = 56 * 1024 * 1024


def _dot(a, b):
    return jnp.dot(a, b, preferred_element_type=F32)


def _dot_nt(a, b):
    return lax.dot_general(a, b, (((1,), (1,)), ((), ())), preferred_element_type=F32)


def _split3(x):
    hi = x.astype(BF16)
    r1 = x - hi.astype(F32)
    mid = r1.astype(BF16)
    lo = (r1 - mid.astype(F32)).astype(BF16)
    return hi, mid, lo


def _sigmoid(x):
    return 1.0 / (1.0 + jnp.exp(-x))


def _params(sem):
    return pltpu.CompilerParams(dimension_semantics=sem, vmem_limit_bytes=VMEM_LIMIT)


def _rope_table_kernel(pos_ref, inv_ref, cos_ref, sin_ref):
    ang = pos_ref[0] * inv_ref[...]
    lane = lax.broadcasted_iota(jnp.int32, ang.shape, 1)
    cos_ref[0] = jnp.cos(ang)
    s = jnp.sin(ang)
    sin_ref[0] = jnp.where(lane < HEAD_DIM // 2, -s, s)


def _rope_tables(pos_f, inv_full):
    b, n, _ = pos_f.shape
    rows = n if n <= 2048 else 2048
    out = jax.ShapeDtypeStruct((b, n, HEAD_DIM), F32)
    return pl.pallas_call(
        _rope_table_kernel,
        grid=(b, n // rows),
        in_specs=[pl.BlockSpec((1, rows, 1), lambda i, j: (i, j, 0)),
                  pl.BlockSpec((1, HEAD_DIM), lambda i, j: (0, 0))],
        out_specs=[pl.BlockSpec((1, rows, HEAD_DIM), lambda i, j: (i, j, 0))] * 2,
        out_shape=[out, out],
        compiler_params=_params(("parallel", "parallel")),
        name="rope_tables",
    )(pos_f, inv_full)


def _proj_kernel(x_ref, nw_ref, w_ref, o_ref, h_ref):
    @pl.when(pl.program_id(1) == 0)
    def _():
        x = x_ref[...]
        ms = jnp.mean(x * x, axis=-1, keepdims=True)
        h_ref[...] = (x * lax.rsqrt(ms + EPS) * nw_ref[...]).astype(BF16)

    o_ref[...] = _dot(h_ref[...], w_ref[...])


def _in_proj(x2d, norm_w, w_pack):
    t = x2d.shape[0]
    return pl.pallas_call(
        _proj_kernel,
        grid=(t // PROJ_TM, D_PACK // PROJ_TN),
        in_specs=[pl.BlockSpec((PROJ_TM, D_MODEL), lambda i, j: (i, 0)),
                  pl.BlockSpec((1, D_MODEL), lambda i, j: (0, 0)),
                  pl.BlockSpec((D_MODEL, PROJ_TN), lambda i, j: (0, j))],
        out_specs=pl.BlockSpec((PROJ_TM, PROJ_TN), lambda i, j: (i, j)),
        out_shape=jax.ShapeDtypeStruct((t, D_PACK), F32),
        scratch_shapes=[pltpu.VMEM((PROJ_TM, D_MODEL), BF16)],
        compiler_params=_params(("parallel", "arbitrary")),
        name="in_proj",
    )(x2d, norm_w, w_pack)


def _prep_kernel(kv_ref, q_ref, cos_ref, sin_ref,
                 qo_ref, kcvc_ref, ks_ref, kw_ref, vst_ref, vwt_ref):
    c = cos_ref[0]
    s = sin_ref[0]

    def rope(u):
        return u * c + pltpu.roll(u, HEAD_DIM // 2, 1) * s

    for h in range(N_HEADS_ATTN):
        sl = slice(HEAD_DIM * h, HEAD_DIM * (h + 1))
        qo_ref[0, h] = (rope(q_ref[:, sl]) * Q_SCALE).T.astype(BF16)

    def seg(i, g):
        lo = (i * N_KV_GROUPS + g) * HEAD_DIM
        return kv_ref[:, lo:lo + HEAD_DIM]

    tq = q_ref.shape[0]
    tok = pl.program_id(1) * tq + lax.broadcasted_iota(jnp.int32, (tq, BLOCK_LANES), 0)
    lane = lax.broadcasted_iota(jnp.int32, (tq, BLOCK_LANES), 1)
    block_onehot = jnp.where((tok // SLC_BLOCK) % BLOCK_LANES == lane, 1.0, 0.0).astype(BF16)

    for g in range(N_KV_GROUPS):
        kcvc_ref[0, 0, g] = seg(0, g).astype(BF16)
        kcvc_ref[0, 1, g] = seg(1, g).astype(BF16)
        ks_ref[0, g, :, 0:HEAD_DIM] = rope(seg(2, g)).astype(BF16)
        ks_ref[0, g, :, HEAD_DIM:HEAD_DIM + BLOCK_LANES] = block_onehot
        vst_ref[0, g] = seg(3, g).T.astype(BF16)
        kw_ref[0, g] = rope(seg(4, g)).astype(BF16)
        vwt_ref[0, g] = seg(5, g).T.astype(BF16)


def _prep(proj, cos_t, sin_t, b, s):
    nq = s // PREP_TQ
    g = N_KV_GROUPS
    row = lambda i, j: (i * nq + j, 0)
    kshape = jax.ShapeDtypeStruct((b, g, s, HEAD_DIM), BF16)
    vshape = jax.ShapeDtypeStruct((b, g, HEAD_DIM, s), BF16)
    kspec = pl.BlockSpec((1, g, PREP_TQ, HEAD_DIM), lambda i, j: (i, 0, j, 0))
    vspec = pl.BlockSpec((1, g, HEAD_DIM, PREP_TQ), lambda i, j: (i, 0, 0, j))
    ks_width = HEAD_DIM + BLOCK_LANES
    ks_shape = jax.ShapeDtypeStruct((b, g, s, ks_width), BF16)
    ks_spec = pl.BlockSpec((1, g, PREP_TQ, ks_width), lambda i, j: (i, 0, j, 0))
    return pl.pallas_call(
        _prep_kernel,
        grid=(b, nq),
        in_specs=[pl.BlockSpec((PREP_TQ, KV_COLS), row),
                  pl.BlockSpec((PREP_TQ, D_MODEL), lambda i, j: (i * nq + j, 1)),
                  pl.BlockSpec((1, PREP_TQ, HEAD_DIM), lambda i, j: (i, j, 0)),
                  pl.BlockSpec((1, PREP_TQ, HEAD_DIM), lambda i, j: (i, j, 0))],
        out_specs=[pl.BlockSpec((1, N_HEADS_ATTN, HEAD_DIM, PREP_TQ), lambda i, j: (i, 0, 0, j)),
                   pl.BlockSpec((1, 2, g, PREP_TQ, HEAD_DIM), lambda i, j: (i, 0, 0, j, 0)),
                   ks_spec, kspec, vspec, vspec],
        out_shape=[jax.ShapeDtypeStruct((b, N_HEADS_ATTN, HEAD_DIM, s), BF16),
                   jax.ShapeDtypeStruct((b, 2, g, s, HEAD_DIM), BF16),
                   ks_shape, kshape, vshape, vshape],
        compiler_params=_params(("parallel", "parallel")),
        name="rope_relayout",
    )(proj, proj, cos_t, sin_t)


def _cmp_kernel(x_ref, w1_ref, w2_ref, pe_ref, cos_ref, sin_ref, kc_ref, vct_ref):
    n = x_ref.shape[3]
    for kv in range(2):
        x = x_ref[0, kv, 0]
        a0 = _dot(x, w1_ref[kv, 0])
        a1 = _dot(x, w1_ref[kv, 1])
        bias = jnp.zeros((8, CMP_HIDDEN), F32)
        for r in range(2):
            pe = jnp.broadcast_to(pe_ref[kv, r], (8, CMP_STRIDE * HEAD_DIM))
            for part in _split3(pe):
                bias = bias + _dot(part, w1_ref[kv, r])
        hid = a0 + pltpu.roll(a1, n - 1, 0) + bias[0:1]
        act = hid * _sigmoid(hid)
        out = _dot(act.astype(BF16), w2_ref[kv])
        if kv == 0:
            roped = out * cos_ref[0] + pltpu.roll(out, HEAD_DIM // 2, 1) * sin_ref[0]
            kc_ref[0, 0] = roped.astype(BF16)
        else:
            vct_ref[0, 0] = out.T.astype(BF16)


def _compress(kcvc, w1, w2, pe, cos_c, sin_c):
    b, _, g, n, width = kcvc.shape
    return pl.pallas_call(
        _cmp_kernel,
        grid=(b, g),
        in_specs=[pl.BlockSpec((1, 2, 1, n, width), lambda i, j: (i, 0, j, 0, 0)),
                  pl.BlockSpec(w1.shape, lambda i, j: (0, 0, 0, 0)),
                  pl.BlockSpec(w2.shape, lambda i, j: (0, 0, 0)),
                  pl.BlockSpec(pe.shape, lambda i, j: (0, 0, 0, 0)),
                  pl.BlockSpec((1, n, HEAD_DIM), lambda i, j: (i, 0, 0)),
                  pl.BlockSpec((1, n, HEAD_DIM), lambda i, j: (i, 0, 0))],
        out_specs=[pl.BlockSpec((1, 1, n, HEAD_DIM), lambda i, j: (i, j, 0, 0)),
                   pl.BlockSpec((1, 1, HEAD_DIM, n), lambda i, j: (i, j, 0, 0))],
        out_shape=[jax.ShapeDtypeStruct((b, g, n, HEAD_DIM), BF16),
                   jax.ShapeDtypeStruct((b, g, HEAD_DIM, n), BF16)],
        compiler_params=_params(("parallel", "parallel")),
        name="compress",
    )(kcvc, w1, w2, pe, cos_c, sin_c)


def _attn_kernel(q_ref, kc_ref, vct_ref, ks_ref, vst_ref, kw_ref, vwt_ref, gd_ref, o_ref,
                 sc_ref, imp_ref, sel_ref, acc_ref, gt_ref, sa_ref, sb_ref, qp_ref):
    n_cmp = kc_ref.shape[2]
    n_blk = imp_ref.shape[0] // (SLC_BLOCK // CMP_STRIDE)
    g = pl.program_id(1)
    qi = pl.program_id(2)
    t0 = qi * Q_BLOCK
    lanes_all = GROUP_WIDTH

    qt = jnp.concatenate([q_ref[0, r] for r in range(HEADS_PER_GROUP)], axis=1)
    t_lane = t0 + lax.broadcasted_iota(jnp.int32, (1, Q_BLOCK), 1)

    def tile_heads(x):
        return jnp.concatenate([x] * HEADS_PER_GROUP, axis=1)

    neg_row = jnp.full((1, lanes_all), NEG_INF, F32)
    zero_row = jnp.zeros((1, lanes_all), F32)

    chunk = min(CMP_CHUNK, n_cmp)
    n_c = (8 * qi + 6) // chunk + 1
    imp_ref[...] = jnp.zeros_like(imp_ref)
    acc_ref[0] = jnp.zeros((HEAD_DIM, lanes_all), F32)

    def cmp_scores(c, carry):
        m, l = carry
        off = pl.multiple_of(c * chunk, chunk)
        ci = off + lax.broadcasted_iota(jnp.int32, (chunk, Q_BLOCK), 0)
        vis = ci * CMP_STRIDE + (CMP_BLOCK - 1) <= t_lane
        s = _dot(kc_ref[0, 0, pl.ds(off, chunk), :], qt) + tile_heads(jnp.where(vis, 0.0, NEG_INF))
        sc_ref[pl.ds(off, chunk), :] = s
        m_new = jnp.maximum(m, jnp.max(s, axis=0, keepdims=True))
        l = l * jnp.exp2(m - m_new) + jnp.sum(jnp.exp2(s - m_new), axis=0, keepdims=True)
        return m_new, l

    def by_pairs(body, init):
        cr = lax.fori_loop(0, n_c // 2, lambda i, cr: body(2 * i + 1, body(2 * i, cr)), init)
        return lax.fori_loop(n_c - n_c % 2, n_c, body, cr)

    m_c, l_c = by_pairs(cmp_scores, (neg_row, zero_row))
    w_c = jnp.where(m_c > 0.5 * NEG_INF, 1.0 / l_c, 0.0)

    def cmp_out(c, carry):
        off = pl.multiple_of(c * chunk, chunk)
        pn = jnp.exp2(sc_ref[pl.ds(off, chunk), :] - m_c) * w_c
        imp = pn[:, 0:Q_BLOCK]
        for r in range(1, HEADS_PER_GROUP):
            imp = imp + pn[:, Q_BLOCK * r:Q_BLOCK * (r + 1)]
        imp_ref[pl.ds(off, chunk), :] = imp
        acc_ref[0] += _dot(vct_ref[0, 0, :, pl.ds(off, chunk)], pn.astype(BF16))
        return carry

    by_pairs(cmp_out, 0)

    ratio = SLC_BLOCK // CMP_STRIDE
    score_in = pltpu.roll(imp_ref[pl.ds(ratio - 1, n_blk, stride=ratio), :], 1, 0)
    for m in range(ratio):
        score_in = score_in + imp_ref[pl.ds(m, n_blk, stride=ratio), :]
    blk = lax.broadcasted_iota(jnp.int32, (n_blk, Q_BLOCK), 0)
    cur = t_lane // SLC_BLOCK
    forced = (blk == 0) | (blk == cur) | (blk == cur - 1)
    visible = blk <= cur
    score0 = jnp.where(forced, -jnp.inf, jnp.where(visible, score_in, -SEL_FORCED))
    blk_f = blk.astype(F32)

    def search(score, sel, ids):
        def pick(_, carry):
            score, sel = carry
            mx = jnp.max(score, axis=0, keepdims=True)
            first = jnp.min(jnp.where(score == mx, ids, float(n_blk)), axis=0, keepdims=True)
            hit = ids == first
            return jnp.where(hit, -jnp.inf, score), jnp.where(hit, 1.0, sel)

        return lax.fori_loop(0, min(N_SELECT, n_blk) - 3, pick, (score, sel))[1]

    sel0 = jnp.where(forced, 1.0, 0.0)
    if n_blk > BLOCK_LANES:
        low = BLOCK_LANES
        sel = lax.cond(
            t0 + Q_BLOCK <= low * SLC_BLOCK,
            lambda: jnp.concatenate([search(score0[:low], sel0[:low], blk_f[:low]), sel0[low:]], axis=0),
            lambda: search(score0, sel0, blk_f))
    else:
        sel = search(score0, sel0, blk_f)
    sel_ref[...] = jnp.zeros_like(sel_ref)
    sel_ref[0:n_blk, :] = jnp.where(visible, sel, 0.0)

    for h in range(qp_ref.shape[0]):
        picked = sel_ref[BLOCK_LANES * h:BLOCK_LANES * (h + 1), :]
        qp_ref[h, 0:HEAD_DIM, :] = qt
        qp_ref[h, HEAD_DIM:HEAD_DIM + BLOCK_LANES, :] = tile_heads(
            jnp.where(picked > 0.5, 0.0, NEG_INF).astype(BF16))

    def finish_scores(s, s_ref, n):
        s_ref[0:n, :] = s
        return jnp.max(s, axis=0, keepdims=True)

    def causal_bias(off, n):
        kpos = off + lax.broadcasted_iota(jnp.int32, (n, Q_BLOCK), 0)
        return tile_heads(jnp.where(kpos <= t_lane, 0.0, NEG_INF))

    def sel_scores(off, s_ref, causal):
        s = _dot(ks_ref[0, 0, pl.ds(off, SEL_CHUNK), :], qp_ref[off // BLOCK_SPAN])
        if causal:
            s = s + causal_bias(off, SEL_CHUNK)
        return finish_scores(s, s_ref, SEL_CHUNK)

    def win_scores(off, s_ref, n):
        kpos = off + lax.broadcasted_iota(jnp.int32, (n, Q_BLOCK), 0)
        in_win = (kpos <= t_lane) & (kpos > t_lane - WINDOW)
        s = _dot(kw_ref[0, 0, pl.ds(off, n), :], qt) + tile_heads(jnp.where(in_win, 0.0, NEG_INF))
        return finish_scores(s, s_ref, n)

    def value_tile(vt, s_ref, n, tile_max, m, l, branch):
        m_new = jnp.maximum(m, tile_max)
        alpha = jnp.exp2(m - m_new)
        p = jnp.exp2(s_ref[0:n, :] - m_new)
        acc_ref[branch] = alpha * acc_ref[branch] + _dot(vt, p.astype(BF16))
        return m_new, alpha * l + jnp.sum(p, axis=0, keepdims=True)

    def sel_values(off, s_ref, tile_max, m, l):
        return value_tile(vst_ref[0, 0, :, pl.ds(off, SEL_CHUNK)], s_ref, SEL_CHUNK, tile_max, m, l, 1)

    trip = 2 * SEL_CHUNK
    n_k = (t0 + Q_BLOCK - 1) // trip + 1
    acc_ref[1] = jnp.zeros((HEAD_DIM, lanes_all), F32)

    def sel_step(c, carry):
        m, l, max_a = carry
        off_a = pl.multiple_of(c * trip, trip)
        off_b = pl.multiple_of(off_a + SEL_CHUNK, SEL_CHUNK)
        max_b = sel_scores(off_b, sb_ref, False)
        m, l = sel_values(off_a, sa_ref, max_a, m, l)
        max_a = sel_scores(pl.multiple_of(off_a + trip, trip), sa_ref, False)
        m, l = sel_values(off_b, sb_ref, max_b, m, l)
        return m, l, max_a

    max_a0 = sel_scores(0, sa_ref, False)
    n_main = n_k - 1
    def trips(first, count, cr):
        for j in range(count):
            cr = sel_step(first + j, cr)
        return cr

    n_quad = n_main // 4
    carry = lax.fori_loop(0, n_quad, lambda i, cr: trips(4 * i, 4, cr), (neg_row, zero_row, max_a0))
    n_pair = (n_main - 4 * n_quad) // 2
    carry = lax.fori_loop(0, n_pair, lambda i, cr: trips(4 * n_quad, 2, cr), carry)
    m_s, l_s, _ = lax.fori_loop(4 * n_quad + 2 * n_pair, n_main, sel_step, carry)

    off_a = pl.multiple_of((n_k - 1) * trip, trip)
    off_b = pl.multiple_of(off_a + SEL_CHUNK, SEL_CHUNK)
    ws = pl.multiple_of(jnp.maximum(t0 - WINDOW, 0), Q_BLOCK)
    max_a = finish_scores(sa_ref[...] + causal_bias(off_a, SEL_CHUNK), sa_ref, SEL_CHUNK)
    max_b = sel_scores(off_b, sb_ref, True)
    m_s, l_s = sel_values(off_a, sa_ref, max_a, m_s, l_s)
    max_w0 = win_scores(ws, sa_ref, SEL_CHUNK)
    m_s, l_s = sel_values(off_b, sb_ref, max_b, m_s, l_s)
    inv_s = 1.0 / l_s

    acc_ref[2] = jnp.zeros((HEAD_DIM, lanes_all), F32)
    last_n = WIN_SPAN - 2 * SEL_CHUNK
    ws1 = pl.multiple_of(ws + SEL_CHUNK, Q_BLOCK)
    ws2 = pl.multiple_of(ws + 2 * SEL_CHUNK, Q_BLOCK)
    max_w1 = win_scores(ws1, sb_ref, SEL_CHUNK)
    m_w, l_w = value_tile(vwt_ref[0, 0, :, pl.ds(ws, SEL_CHUNK)], sa_ref, SEL_CHUNK, max_w0, neg_row, zero_row, 2)
    max_w2 = win_scores(ws2, sa_ref, last_n)
    m_w, l_w = value_tile(vwt_ref[0, 0, :, pl.ds(ws1, SEL_CHUNK)], sb_ref, SEL_CHUNK, max_w1, m_w, l_w, 2)
    m_w, l_w = value_tile(vwt_ref[0, 0, :, pl.ds(ws2, last_n)], sa_ref, last_n, max_w2, m_w, l_w, 2)
    inv_w = 1.0 / l_w

    gt_ref[...] = _sigmoid(gd_ref[...]).T
    for r in range(HEADS_PER_GROUP):
        sl = slice(Q_BLOCK * r, Q_BLOCK * (r + 1))
        base = 3 * (HEADS_PER_GROUP * g + r)
        o_t = (gt_ref[pl.ds(base, 1), :] * acc_ref[0, :, sl]
               + gt_ref[pl.ds(base + 1, 1), :] * (acc_ref[1, :, sl] * inv_s[:, sl])
               + gt_ref[pl.ds(base + 2, 1), :] * (acc_ref[2, :, sl] * inv_w[:, sl]))
        o_ref[0, :, HEAD_DIM * r:HEAD_DIM * (r + 1)] = o_t.T


def _attention(q_r, k_cmp, v_cmp_t, k_slc, v_slc_t, k_win, v_win_t, proj, b, s):
    g = N_KV_GROUPS
    nqb = s // Q_BLOCK
    n_cmp = s // CMP_STRIDE
    n_blk = s // SLC_BLOCK
    n_bias = pl.cdiv(n_blk, BLOCK_LANES)
    once = pl.Buffered(1)
    kspec = lambda n, w: pl.BlockSpec((1, 1, n, w), lambda i, j, k: (i, j, 0, 0), pipeline_mode=once)
    vspec = lambda n: pl.BlockSpec((1, 1, HEAD_DIM, n), lambda i, j, k: (i, j, 0, 0), pipeline_mode=once)
    return pl.pallas_call(
        _attn_kernel,
        grid=(b, g, nqb),
        in_specs=[pl.BlockSpec((1, HEADS_PER_GROUP, HEAD_DIM, Q_BLOCK), lambda i, j, k: (i, j, 0, k)),
                  kspec(n_cmp, HEAD_DIM), vspec(n_cmp), kspec(s, HEAD_DIM + BLOCK_LANES), vspec(s),
                  kspec(s, HEAD_DIM), vspec(s),
                  pl.BlockSpec((Q_BLOCK, 128), lambda i, j, k: (i * nqb + k, GD_BLOCK))],
        out_specs=pl.BlockSpec((1, Q_BLOCK, GROUP_WIDTH), lambda i, j, k: (i, k, j)),
        out_shape=jax.ShapeDtypeStruct((b, s, D_MODEL), F32),
        scratch_shapes=[pltpu.VMEM((n_cmp, GROUP_WIDTH), F32),
                        pltpu.VMEM((n_cmp, Q_BLOCK), F32),
                        pltpu.VMEM((n_bias * BLOCK_LANES, Q_BLOCK), F32),
                        pltpu.VMEM((3, HEAD_DIM, GROUP_WIDTH), F32),
                        pltpu.VMEM((128, Q_BLOCK), F32),
                        pltpu.VMEM((SEL_CHUNK, GROUP_WIDTH), F32),
                        pltpu.VMEM((SEL_CHUNK, GROUP_WIDTH), F32),
                        pltpu.VMEM((n_bias, HEAD_DIM + BLOCK_LANES, GROUP_WIDTH), BF16)],
        compiler_params=_params(("parallel", "parallel", "arbitrary")),
        name="nsa_attention",
    )(q_r, k_cmp, v_cmp_t, k_slc, v_slc_t, k_win, v_win_t, proj)


def _ssm_kernel(xbc_ref, gd_ref, z_ref, cw_ref, cb_ref, dtb_ref, alog_ref, dsk_ref, nw_ref, tri_ref,
                y_ref, tail_ref, st_ref, u_ref):
    L = SSD_CHUNK

    @pl.when(pl.program_id(1) == 0)
    def _():
        tail_ref[...] = jnp.zeros_like(tail_ref)
        st_ref[...] = jnp.zeros_like(st_ref)

    strip = 512
    for cs in range(CONV_DIM // strip):
        sl = slice(strip * cs, strip * (cs + 1))
        cur = xbc_ref[:, sl]
        ext = jnp.concatenate([tail_ref[:, sl], cur], axis=0)
        acc = cb_ref[:, sl] + cw_ref[0:1, sl] * ext[5:5 + L]
        for k in range(1, CONV_WIDTH):
            acc = acc + cw_ref[k:k + 1, sl] * ext[5 + k:5 + k + L]
        tail_ref[:, sl] = cur[L - 8:L]
        u_ref[:, sl] = acc * _sigmoid(acc)

    x_dt = gd_ref[...] + dtb_ref[...]
    dt = jnp.maximum(x_dt, 0.0) + jnp.log1p(jnp.exp(-jnp.abs(x_dt)))
    da = dt * -jnp.exp(alog_ref[...])
    tri = tri_ref[...]
    cum = jnp.zeros((L, 128), F32)
    for part in _split3(da):
        cum = cum + _dot(tri, part)
    cum_t = cum.T
    row = lax.broadcasted_iota(jnp.int32, (L, L), 0)
    col = lax.broadcasted_iota(jnp.int32, (L, L), 1)
    causal = row >= col

    last = cum[L - 1:L, :]
    per_head = jnp.concatenate(
        [dt, dt * jnp.exp(last - cum), jnp.exp(cum), jnp.broadcast_to(jnp.exp(last), (8, 128))], axis=0)
    per_head_parts = _split3(per_head)
    gw = SSM_HEADS_PER_GROUP * SSM_HEAD_DIM
    src_lane = lax.broadcasted_iota(jnp.int32, (128, gw), 0)
    head_of_lane = lax.broadcasted_iota(jnp.int32, (128, gw), 1) // SSM_HEAD_DIM
    out_head = lax.broadcasted_iota(jnp.int32, (L, gw), 1) // SSM_HEAD_DIM

    for g in range(N_SSM_GROUPS):
        b_g = u_ref[:, D_SSM + D_STATE * g:D_SSM + D_STATE * (g + 1)]
        c_g = u_ref[:, D_SSM + N_SSM_GROUPS * D_STATE + D_STATE * g:
                    D_SSM + N_SSM_GROUPS * D_STATE + D_STATE * (g + 1)]
        b_b = b_g.astype(BF16)
        c_b = c_g.astype(BF16)
        cb = _dot_nt(c_b, b_b)
        b_t = b_g.T.astype(BF16)
        st = st_ref[g]
        sl = slice(gw * g, gw * (g + 1))
        x_g = u_ref[:, sl]

        spread = jnp.where(src_lane == DT_LANE0 + SSM_HEADS_PER_GROUP * g + head_of_lane, 1.0, 0.0).astype(BF16)
        cols = _dot(per_head_parts[0], spread)
        for part in per_head_parts[1:]:
            cols = cols + _dot(part, spread)
        xdt = x_g * cols[0:L]
        x_end = x_g * cols[L:2 * L]
        y_g = _dot(c_b, st.astype(BF16)) * cols[2 * L:3 * L] + dsk_ref[:, sl] * x_g
        for r in range(SSM_HEADS_PER_GROUP):
            lane = DT_LANE0 + SSM_HEADS_PER_GROUP * g + r
            cc = cum[:, lane:lane + 1]
            cr = cum_t[lane:lane + 1, :]
            decay = jnp.exp(jnp.where(causal, cc - cr, -jnp.inf))
            w = (cb * decay).astype(BF16)
            y_g = y_g + _dot(w, jnp.where(out_head == r, xdt, 0.0).astype(BF16))
        st_ref[g] = st * cols[3 * L:3 * L + 1] + _dot(b_t, x_end.astype(BF16))
        z_g = z_ref[:, sl]
        v = y_g * (z_g * _sigmoid(z_g))
        ms = jnp.mean(v * v, axis=-1, keepdims=True)
        y_ref[:, sl] = (v * lax.rsqrt(ms + EPS) * nw_ref[:, sl]).astype(BF16)


def _ssm(proj, conv_w, conv_b, dtb, alog, dsk, norm_w, tri, b, s):
    nc = s // SSD_CHUNK
    full = lambda shape: pl.BlockSpec(shape, lambda i, j: (0,) * len(shape))
    return pl.pallas_call(
        _ssm_kernel,
        grid=(b, nc),
        in_specs=[pl.BlockSpec((SSD_CHUNK, CONV_DIM), lambda i, j: (i * nc + j, 1)),
                  pl.BlockSpec((SSD_CHUNK, 128), lambda i, j: (i * nc + j, GD_BLOCK)),
                  pl.BlockSpec((SSD_CHUNK, D_SSM), lambda i, j: (i * nc + j, 5)),
                  full((CONV_WIDTH, CONV_DIM)), full((1, CONV_DIM)),
                  full((1, 128)), full((1, 128)), full((1, D_SSM)),
                  full((1, D_SSM)), full((SSD_CHUNK, SSD_CHUNK))],
        out_specs=pl.BlockSpec((SSD_CHUNK, D_SSM), lambda i, j: (i * nc + j, 0)),
        out_shape=jax.ShapeDtypeStruct((b * s, D_SSM), BF16),
        scratch_shapes=[pltpu.VMEM((8, CONV_DIM), F32),
                        pltpu.VMEM((N_SSM_GROUPS, D_STATE, SSM_HEADS_PER_GROUP * SSM_HEAD_DIM), F32),
                        pltpu.VMEM((SSD_CHUNK, CONV_DIM), F32)],
        compiler_params=_params(("parallel", "arbitrary")),
        name="ssd_scan",
    )(proj, proj, proj, conv_w, conv_b, dtb, alog, dsk, norm_w, tri)


def _out_kernel(o_ref, z_ref, ys_ref, x_ref, naw_ref, w_ref, fw_ref, out_ref, *, final):
    z = z_ref[...]
    v = o_ref[...] * (z * _sigmoid(z))
    ms = jnp.mean(v * v, axis=-1, keepdims=True)
    ya = (v * lax.rsqrt(ms + EPS) * naw_ref[...]).astype(BF16)
    acc = x_ref[...] + _dot(ya, w_ref[0:D_MODEL, :]) + _dot(ys_ref[...], w_ref[D_MODEL:2 * D_MODEL, :])
    if final:
        ms2 = jnp.mean(acc * acc, axis=-1, keepdims=True)
        acc = acc * lax.rsqrt(ms2 + EPS) * fw_ref[...]
    out_ref[...] = acc


def _out_proj(o_attn, proj, y_ssm, x2d, norm_attn_w, w_out, final_w, final):
    t = x2d.shape[0]
    rowblk = lambda c: pl.BlockSpec((OUT_TM, D_MODEL), lambda i: (i, c))
    vec = pl.BlockSpec((1, D_MODEL), lambda i: (0, 0))
    return pl.pallas_call(
        functools.partial(_out_kernel, final=final),
        grid=(t // OUT_TM,),
        in_specs=[rowblk(0), rowblk(4), rowblk(0), rowblk(0), vec,
                  pl.BlockSpec((2 * D_MODEL, D_MODEL), lambda i: (0, 0), pipeline_mode=pl.Buffered(1)),
                  vec],
        out_specs=rowblk(0),
        out_shape=jax.ShapeDtypeStruct((t, D_MODEL), F32),
        compiler_params=_params(("parallel",)),
        name="out_proj",
    )(o_attn, proj, y_ssm, x2d, norm_attn_w, w_out, final_w)


def _pack_w_in(w):
    q0, kv0, gate0, za0, xbc0, dt0, zs0, end = 0, 2048, 3584, 3632, 5680, 9776, 9808, 11856
    w = w.astype(BF16)
    pad = jnp.zeros((w.shape[0], 2048 - KV_COLS - GATE_LANES - N_HEADS_SSM), BF16)
    return jnp.concatenate(
        [w[:, kv0:gate0], w[:, gate0:za0], w[:, dt0:zs0], pad,
         w[:, q0:kv0], w[:, xbc0:dt0], w[:, za0:xbc0], w[:, zs0:end]], axis=1)


def _dt_lanes(v):
    return jnp.zeros((1, 128), F32).at[0, DT_LANE0:DT_LANE0 + N_HEADS_SSM].set(v.astype(F32))


def kernel(x, positions, norm_w, w_in, cmp_pe_k, cmp_w1_k, cmp_w2_k, cmp_pe_v, cmp_w1_v, cmp_w2_v,
           conv_w, conv_b, dt_bias, a_log, d_skip, norm_attn_w, norm_ssm_w, w_out, final_norm_w):
    b, s, d = x.shape
    depth = w_in.shape[0]
    assert d == D_MODEL and s % 2048 == 0
    n_chunks = s // CMP_STRIDE

    half = HEAD_DIM // 2
    inv_freq = ROPE_THETA ** (-jnp.arange(half, dtype=F32) / half)
    inv_full = jnp.concatenate([inv_freq, inv_freq])[None, :]
    pos_f = positions.astype(F32)
    cos_t, sin_t = _rope_tables(pos_f[:, :, None], inv_full)
    pos_c = jnp.concatenate([pos_f[:, CMP_BLOCK - 1::CMP_STRIDE], pos_f[:, -1:]], axis=1)
    cos_c, sin_c = _rope_tables(pos_c[:, :, None], inv_full)

    tri = jnp.tril(jnp.ones((SSD_CHUNK, SSD_CHUNK), F32)).astype(BF16)
    x2d = x.reshape(b * s, d)
    for l in range(depth):
        w_pack = _pack_w_in(w_in[l])
        w1 = jnp.stack([cmp_w1_k[l], cmp_w1_v[l]]).reshape(2, 2, CMP_STRIDE * HEAD_DIM, CMP_HIDDEN).astype(BF16)
        w2 = jnp.stack([cmp_w2_k[l], cmp_w2_v[l]]).astype(BF16)
        pe = jnp.stack([cmp_pe_k[l], cmp_pe_v[l]]).reshape(2, 2, 1, CMP_STRIDE * HEAD_DIM)

        proj = _in_proj(x2d, norm_w[l][None, :], w_pack)
        q_r, kcvc, k_slc, k_win, v_slc_t, v_win_t = _prep(proj, cos_t, sin_t, b, s)
        kcvc = kcvc.reshape(b, 2, N_KV_GROUPS, n_chunks, CMP_STRIDE * HEAD_DIM)
        k_cmp, v_cmp_t = _compress(kcvc, w1, w2, pe, cos_c, sin_c)
        o_attn = _attention(q_r, k_cmp, v_cmp_t, k_slc, v_slc_t, k_win, v_win_t, proj, b, s)
        y_ssm = _ssm(proj, conv_w[l], conv_b[l][None, :], _dt_lanes(dt_bias[l]),
                     _dt_lanes(a_log[l]), jnp.repeat(d_skip[l].astype(F32), SSM_HEAD_DIM)[None, :],
                     norm_ssm_w[l][None, :], tri, b, s)
        x2d = _out_proj(o_attn.reshape(b * s, d), proj, y_ssm, x2d, norm_attn_w[l][None, :],
                        w_out[l].astype(BF16), final_norm_w[None, :], final=(l == depth - 1))
    return x2d.reshape(b, s, d)
```
